```python
import math
import jax
import jax.numpy as jnp
from jax import lax
import numpy as np


D_MODEL = 2048
BATCH = 1
SEQ = 8192
DEPTH = 2

GRID_W = 64
CTX_LEN = 256
HEAD_DIM = 64
N_BRANCH = 4
BRANCH_WIDTH = D_MODEL // 4

ATT_HEADS = BRANCH_WIDTH // HEAD_DIM
ATT_KV_HEADS = ATT_HEADS // 4
ATT_GROUP = ATT_HEADS // ATT_KV_HEADS
WINDOW = 128
ATT_BLOCK = 128
ROPE_BASE = 10000.0

SSM_GROUP = 16
SSM_GROUPS = BRANCH_WIDTH // SSM_GROUP
SSM_STATE = 64

RWKV_HEADS = BRANCH_WIDTH // HEAD_DIM
DECAY_LORA = 64
ICLR_LORA = 64
GATE_LORA = 128
RWKV_GN_EPS = 64e-5

HGRN_HEADS = BRANCH_WIDTH // HEAD_DIM
HGRN_CHUNK = 64

N_EXPERTS = 32
TOP_K = 4
D_EXPERT = D_MODEL // 2
SWIGLU_LIMIT = 7.0
SWIGLU_ALPHA = 1.702
MOE_BLOCK = 128

NORM_EPS = 1e-6

ATT_COLS = (ATT_HEADS * HEAD_DIM, ATT_KV_HEADS * HEAD_DIM, ATT_KV_HEADS * HEAD_DIM)
RWKV_COLS = (BRANCH_WIDTH, BRANCH_WIDTH, BRANCH_WIDTH, DECAY_LORA, DECAY_LORA, ICLR_LORA, ICLR_LORA, GATE_LORA)
HGRN_COLS = (BRANCH_WIDTH,) * 5
IN_COLS = ATT_COLS + (BRANCH_WIDTH, sum(RWKV_COLS), sum(HGRN_COLS), N_BRANCH * D_MODEL)
D_IN = sum(IN_COLS)

kernel_name = 'hybrid_diffusion_gated_merge_moe'


def split_cols(p, sizes):
    idx = [sum(sizes[:i + 1]) for i in range(len(sizes) - 1)]
    return jnp.split(p, idx, axis=-1)


def rms_norm(x, w):
    xf = x.astype(jnp.float32)
    y = xf * lax.rsqrt(jnp.mean(xf * xf, axis=-1, keepdims=True) + NORM_EPS)
    return (y * w).astype(x.dtype)


def modulate(h, shift, scale):
    return h * (1.0 + scale) + shift


def axial_rope(rows):
    n_freq = HEAD_DIM // 4
    inv = ROPE_BASE ** (-jnp.arange(n_freq, dtype=jnp.float32) / n_freq)
    row = jnp.repeat(jnp.arange(rows, dtype=jnp.float32), GRID_W)
    col = jnp.tile(jnp.arange(GRID_W, dtype=jnp.float32), rows)
    ang = jnp.concatenate([row[:, None] * inv, col[:, None] * inv], axis=-1)
    return jnp.cos(ang), jnp.sin(ang)


def apply_rope(x, cos, sin):
    xf = x.astype(jnp.float32)
    half = HEAD_DIM // 2
    x1, x2 = xf[..., :half], xf[..., half:]
    cs, sn = cos[None, :, None, :], sin[None, :, None, :]
    return jnp.concatenate([x1 * cs - x2 * sn, x1 * sn + x2 * cs], axis=-1).astype(x.dtype)


def softmax_with_sink(logits, sink):
    m = jnp.maximum(jnp.max(logits, axis=-1, keepdims=True), sink)
    e = jnp.exp(logits - m)
    return e / (jnp.sum(e, axis=-1, keepdims=True) + jnp.exp(sink - m))


def attention_branch(q_c, k_c, v_c, q_l, k_l, v_l, qn_w, kn_w, sink, cos, sin, need_ctx):
    B, S = q_l.shape[:2]
    C = q_c.shape[1]
    heads = lambda t, h: t.reshape(t.shape[0], t.shape[1], h, HEAD_DIM)
    kc = rms_norm(heads(k_c, ATT_KV_HEADS), kn_w)
    vc = heads(v_c, ATT_KV_HEADS)
    ql = apply_rope(rms_norm(heads(q_l, ATT_HEADS), qn_w), cos, sin)
    kl = apply_rope(rms_norm(heads(k_l, ATT_KV_HEADS), kn_w), cos, sin)
    vl = heads(v_l, ATT_KV_HEADS)
    scale = HEAD_DIM ** -0.5
    sink_f = sink.astype(jnp.float32).reshape(ATT_KV_HEADS, ATT_GROUP)
    nb = S // ATT_BLOCK
    qb = ql.reshape(B, nb, ATT_BLOCK, ATT_KV_HEADS, ATT_GROUP, HEAD_DIM)

    def band(t):
        tp = jnp.pad(t, ((0, 0), (ATT_BLOCK, ATT_BLOCK), (0, 0), (0, 0)))
        tp = tp.reshape(B, nb + 2, ATT_BLOCK, ATT_KV_HEADS, HEAD_DIM)
        return jnp.concatenate([tp[:, :-2], tp[:, 1:-1], tp[:, 2:]], axis=2)

    kb, vb = band(kl), band(vl)
    s_win = jnp.einsum('bnqhgd,bnkhd->bnhgqk', qb, kb, preferred_element_type=jnp.float32) * scale
    q_pos = jnp.arange(nb)[:, None] * ATT_BLOCK + jnp.arange(ATT_BLOCK)[None, :]
    k_pos = jnp.arange(nb)[:, None] * ATT_BLOCK + jnp.arange(3 * ATT_BLOCK)[None, :] - ATT_BLOCK
    kp = k_pos[:, None, :]
    valid = (jnp.abs(kp - q_pos[:, :, None]) <= WINDOW) & (kp >= 0) & (kp < S)
    s_win = jnp.where(valid[None, :, None, None], s_win, -jnp.inf)
    s_ctx = jnp.einsum('bnqhgd,bkhd->bnhgqk', qb, kc, preferred_element_type=jnp.float32) * scale
    p = softmax_with_sink(jnp.concatenate([s_win, s_ctx], axis=-1), sink_f[None, None, :, :, None, None])
    p = p.astype(vl.dtype)
    o = (jnp.einsum('bnhgqk,bnkhd->bnqhgd', p[..., :3 * ATT_BLOCK], vb)
         + jnp.einsum('bnhgqk,bkhd->bnqhgd', p[..., 3 * ATT_BLOCK:], vc))
    lat = o.reshape(B, S, ATT_HEADS * HEAD_DIM)
    ctx_out = None
    if need_ctx:
        qc = rms_norm(heads(q_c, ATT_HEADS), qn_w).reshape(B, C, ATT_KV_HEADS, ATT_GROUP, HEAD_DIM)
        s_cc = jnp.einsum('bqhgd,bkhd->bhgqk', qc, kc, preferred_element_type=jnp.float32) * scale
        pc = softmax_with_sink(s_cc, sink_f[None, :, :, None, None]).astype(vc.dtype)
        ctx_out = jnp.einsum('bhgqk,bkhd->bqhgd', pc, vc).reshape(B, C, ATT_HEADS * HEAD_DIM)
    return ctx_out, lat


def cmul(ar, ai, br, bi):
    return ar * br - ai * bi, ar * bi + ai * br


def diag_scan(a_re, a_im, b_re, b_im, h_re, h_im, reverse):
    a_re = jnp.broadcast_to(a_re, b_re.shape)
    a_im = jnp.broadcast_to(a_im, b_re.shape)

    def combine(e1, e2):
        a1r, a1i, x1r, x1i = e1
        a2r, a2i, x2r, x2i = e2
        ar, ai = cmul(a2r, a2i, a1r, a1i)
        xr, xi = cmul(a2r, a2i, x1r, x1i)
        return ar, ai, xr + x2r, xi + x2i

    pr, pi_, xr, xi = lax.associative_scan(combine, (a_re, a_im, b_re, b_im), reverse=reverse, axis=1)
    hr, hi = cmul(pr, pi_, h_re[:, None], h_im[:, None])
    return xr + hr, xi + hi


def s5_branch(u_c, u_l, lam_re, lam_im, log_dt, b_re, b_im, c_re, c_im, d_skip, glu_w, glu_b, need_ctx):
    f32 = jnp.float32
    dt = jnp.exp(log_dt.astype(f32))[..., None]
    lr, li = lam_re.astype(f32), lam_im.astype(f32)
    mag = jnp.exp(dt * lr)
    ab_re, ab_im = mag * jnp.cos(dt * li), mag * jnp.sin(dt * li)
    den = lr * lr + li * li
    gm_re = ((ab_re - 1.0) * lr + ab_im * li) / den
    gm_im = (ab_im * lr - (ab_re - 1.0) * li) / den

    def drive(u):
        ug = u.astype(f32).reshape(u.shape[0], u.shape[1], SSM_GROUPS, SSM_GROUP)
        return (jnp.einsum('btgc,gpc->btgp', ug, b_re.astype(f32)),
                jnp.einsum('btgc,gpc->btgp', ug, b_im.astype(f32)))

    def readout(xr, xi, u):
        y = (jnp.einsum('btgp,gcp->btgc', xr, c_re.astype(f32))
             - jnp.einsum('btgp,gcp->btgc', xi, c_im.astype(f32)))
        y = y.reshape(u.shape) + d_skip * u.astype(f32)
        z = jax.nn.gelu(y) @ glu_w + glu_b
        za, zb = jnp.split(z, 2, axis=-1)
        return (za * jax.nn.sigmoid(zb)).astype(u.dtype)

    bc_re, bc_im = drive(u_c)
    bl_re, bl_im = drive(u_l)
    B = u_l.shape[0]
    zero = jnp.zeros((B, SSM_GROUPS, SSM_STATE), f32)
    xc_sum, xl_sum = [], []
    for d, reverse in enumerate((False, True)):
        end = 0 if reverse else -1
        inc = cmul(gm_re[d], gm_im[d], bc_re, bc_im)
        xcr, xci = diag_scan(ab_re[d], ab_im[d], inc[0], inc[1], zero, zero, reverse)
        inl = cmul(gm_re[d], gm_im[d], bl_re, bl_im)
        xlr, xli = diag_scan(ab_re[d], ab_im[d], inl[0], inl[1], xcr[:, end], xci[:, end], reverse)
        xc_sum.append((xcr, xci))
        xl_sum.append((xlr, xli))
    lat = readout(xl_sum[0][0] + xl_sum[1][0], xl_sum[0][1] + xl_sum[1][1], u_l)
    ctx_out = None
    if need_ctx:
        ctx_out = readout(xc_sum[0][0] + xc_sum[1][0], xc_sum[0][1] + xc_sum[1][1], u_c)
    return ctx_out, lat


def shift_mix(p, mu_prev, mu_next):
    zero = jnp.zeros_like(p[:, :1])
    prev = jnp.concatenate([zero, p[:, :-1]], axis=1)
    nxt = jnp.concatenate([p[:, 1:], zero], axis=1)
    return p + mu_prev * (prev - p) + mu_next * (nxt - p)


def rwkv_scan(r, w, k, v, kk, a, s0, reverse):
    def step(S, inp):
        r_t, w_t, k_t, v_t, kk_t, a_t = inp
        sa = jnp.einsum('bhvk,bhk->bhv', S, -kk_t)
        S = (S * w_t[:, :, None, :] + sa[..., None] * (kk_t * a_t)[:, :, None, :]
             + v_t[..., None] * k_t[:, :, None, :])
        return S, jnp.einsum('bhvk,bhk->bhv', S, r_t)

    xs = tuple(jnp.moveaxis(t, 1, 0) for t in (r, w, k, v, kk, a))
    S, ys = lax.scan(step, s0, xs, reverse=reverse)
    return S, jnp.moveaxis(ys, 0, 1)


def rwkv_branch(p_c, p_l, mu_prev, mu_next, w0, w2, a0, a2, g2, k_k, k_a, r_k, lnx_w, lnx_b, need_ctx):
    f32 = jnp.float32
    W = BRANCH_WIDTH

    def features(p):
        B, T = p.shape[:2]
        hd = lambda t: t.reshape(B, T, RWKV_HEADS, HEAD_DIM)
        pm = shift_mix(p.astype(f32), mu_prev, mu_next)
        r, k, v, w1f, w1b, a1f, a1b, g1 = split_cols(pm, RWKV_COLS)
        kk = hd(k * k_k)
        kk = kk * lax.rsqrt(jnp.maximum(jnp.sum(kk * kk, axis=-1, keepdims=True), 1e-12))
        dirs = []
        for d, (w1, a1) in enumerate(((w1f, a1f), (w1b, a1b))):
            w = -jax.nn.softplus(-(w0[d] + jnp.tanh(w1) @ w2[d])) - 0.5
            decay = jnp.exp(-jnp.exp(w))
            a = jax.nn.sigmoid(a0[d] + a1 @ a2[d])
            k_d = k * (1.0 + (a - 1.0) * k_a)
            dirs.append((hd(decay), hd(k_d), hd(a)))
        g = jax.nn.sigmoid(g1) @ g2
        return hd(r), hd(v), kk, dirs, g

    def readout(y, r, v, dirs, g, dtype):
        B, T = y.shape[:2]
        mu = jnp.mean(y, axis=-1, keepdims=True)
        var = jnp.mean(jnp.square(y - mu), axis=-1, keepdims=True)
        yn = ((y - mu) * lax.rsqrt(var + RWKV_GN_EPS)).reshape(B, T, W) * lnx_w + lnx_b
        bonus = jnp.sum(r * (dirs[0][1] + dirs[1][1]) * r_k, axis=-1, keepdims=True) * v
        return ((yn + bonus.reshape(B, T, W)) * g).astype(dtype)

    rc, vc, kkc, dirs_c, gc = features(p_c)
    rl, vl, kkl, dirs_l, gl = features(p_l)
    B = p_l.shape[0]
    s0 = jnp.zeros((B, RWKV_HEADS, HEAD_DIM, HEAD_DIM), f32)
    yc, yl = [], []
    for d, reverse in enumerate((False, True)):
        dec_c, kd_c, a_c = dirs_c[d]
        sc, ydc = rwkv_scan(rc, dec_c, kd_c, vc, kkc, a_c, s0, reverse)
        dec_l, kd_l, a_l = dirs_l[d]
        _, ydl = rwkv_scan(rl, dec_l, kd_l, vl, kkl, a_l, sc, reverse)
        yc.append(ydc)
        yl.append(ydl)
    lat = readout(yl[0] + yl[1], rl, vl, dirs_l, gl, p_l.dtype)
    ctx_out = readout(yc[0] + yc[1], rc, vc, dirs_c, gc, p_c.dtype) if need_ctx else None
    return ctx_out, lat


def hgrn_chunk_scan(q, k, v, log_f, s0, reverse):
    if reverse:
        q, k, v, log_f = (jnp.flip(t, axis=1) for t in (q, k, v, log_f))
    B, T, H, _ = q.shape
    Dv = v.shape[-1]
    nc = T // HGRN_CHUNK
    chunk = lambda t: t.reshape(B, nc, HGRN_CHUNK, H, t.shape[-1]).transpose(1, 0, 3, 2, 4)
    causal = jnp.tril(jnp.ones((HGRN_CHUNK, HGRN_CHUNK), dtype=bool))

    def step(S, inp):
        qc, kc, vc, lf = inp
        b = jnp.cumsum(lf, axis=2)
        o = jnp.einsum('bhtd,bhdv->bhtv', qc * jnp.exp(b), S)
        diff = jnp.where(causal[:, :, None], b[:, :, :, None, :] - b[:, :, None, :, :], -jnp.inf)
        att = jnp.einsum('bhtd,bhsd,bhtsd->bhts', qc, kc, jnp.exp(diff))
        o = o + jnp.einsum('bhts,bhsv->bhtv', att, vc)
        b_end = b[:, :, -1:]
        S = (jnp.exp(b_end[:, :, 0])[..., None] * S
             + jnp.einsum('bhsd,bhsv->bhdv', kc * jnp.exp(b_end - b), vc))
        return S, o

    S, o = lax.scan(step, s0, (chunk(q), chunk(k), chunk(v), chunk(log_f)))
    o = o.transpose(1, 0, 3, 2, 4).reshape(B, T, H, Dv)
    if reverse:
        o = jnp.flip(o, axis=1)
    return S, o


def hgrn_branch(p_c, p_l, lb, norm_w, need_ctx):
    f32 = jnp.float32
    log_lb, log_ub = jnp.log(lb), jnp.log1p(-lb)

    def features(p):
        B, T = p.shape[:2]
        hd = lambda t: t.reshape(B, T, HGRN_HEADS, HEAD_DIM)
        q, zf, zb, i, g = split_cols(p.astype(f32), HGRN_COLS)
        gates = []
        for z in (zf, zb):
            log_f = jnp.logaddexp(log_lb, log_ub + jax.nn.log_sigmoid(z))
            k = (1.0 - lb) * jax.nn.sigmoid(-z)
            gates.append((hd(log_f), hd(k)))
        return hd(q), hd(i), gates, g

    def readout(o, g, dtype):
        B, T = o.shape[:2]
        return (rms_norm(o, norm_w).reshape(B, T, BRANCH_WIDTH) * jax.nn.silu(g)).astype(dtype)

    qc, ic, gates_c, gc = features(p_c)
    ql, il, gates_l, gl = features(p_l)
    B = p_l.shape[0]
    s0 = jnp.zeros((B, HGRN_HEADS, HEAD_DIM, HEAD_DIM), f32)
    oc, ol = [], []
    for d, reverse in enumerate((False, True)):
        sc, odc = hgrn_chunk_scan(qc, gates_c[d][1], ic, gates_c[d][0], s0, reverse)
        _, odl = hgrn_chunk_scan(ql, gates_l[d][1], il, gates_l[d][0], sc, reverse)
        oc.append(odc)
        ol.append(odl)
    lat = readout(ol[0] + ol[1], gl, p_l.dtype)
    ctx_out = readout(oc[0] + oc[1], gc, p_c.dtype) if need_ctx else None
    return ctx_out, lat


def gated_merge(gate_cols, branches, b_merge, w_branch, w_out):
    B, T = gate_cols.shape[:2]
    gates = jax.nn.sigmoid(gate_cols + b_merge).reshape(B, T, N_BRANCH, D_MODEL)
    br = jnp.stack(branches, axis=2)
    proj = jnp.einsum('btnc,ncd->btnd', br, w_branch)
    return jnp.sum(gates * proj, axis=2) @ w_out


def moe_ffn(h, router_w, router_b, gu_w, gu_b, dn_w, dn_b):
    N, D = h.shape
    logits = (h @ router_w + router_b).astype(jnp.float32)
    top_l, top_e = lax.top_k(logits, TOP_K)
    weights = jax.nn.softmax(top_l, axis=-1)
    n_assign = N * TOP_K
    flat_e = top_e.reshape(-1)
    order = jnp.argsort(flat_e)
    e_sorted = flat_e[order]
    tok_sorted = (jnp.arange(n_assign, dtype=jnp.int32) // TOP_K)[order]
    w_sorted = weights.reshape(-1)[order].astype(h.dtype)
    counts = jnp.bincount(flat_e, length=N_EXPERTS)
    padded = (counts + MOE_BLOCK - 1) // MOE_BLOCK * MOE_BLOCK
    start_sorted = jnp.cumsum(counts) - counts
    pad_end = jnp.cumsum(padded)
    start_pad = pad_end - padded
    dest = start_pad[e_sorted] + jnp.arange(n_assign, dtype=jnp.int32) - start_sorted[e_sorted]
    n_blocks = -(-(n_assign + N_EXPERTS * (MOE_BLOCK - 1)) // MOE_BLOCK)
    cap = n_blocks * MOE_BLOCK
    slot_tok = jnp.full((cap,), N, dtype=jnp.int32).at[dest].set(tok_sorted)
    slot_w = jnp.zeros((cap,), h.dtype).at[dest].set(w_sorted)
    block_e = jnp.minimum(jnp.searchsorted(pad_end, jnp.arange(n_blocks, dtype=jnp.int32) * MOE_BLOCK, side='right'),
                          N_EXPERTS - 1)
    h_pad = jnp.concatenate([h, jnp.zeros((1, D), h.dtype)], axis=0)
    xb = h_pad[slot_tok].reshape(n_blocks, MOE_BLOCK, D)

    def expert_block(args):
        xblk, e = args
        gu = xblk @ gu_w[e] + gu_b[e]
        gate, up = gu[:, ::2], gu[:, 1::2]
        gate = jnp.minimum(gate, SWIGLU_LIMIT)
        up = jnp.clip(up, -SWIGLU_LIMIT, SWIGLU_LIMIT)
        glu = gate * jax.nn.sigmoid(SWIGLU_ALPHA * gate)
        return ((up + 1.0) * glu) @ dn_w[e] + dn_b[e]

    yb = lax.map(expert_block, (xb, block_e)).reshape(cap, D)
    y = jnp.zeros((N + 1, D), h.dtype).at[slot_tok].add(yb * slot_w[:, None])
    return y[:N]


def setup_inputs(seed: int = 0) -> dict:
    key = jax.random.key(seed)
    ks = iter(jax.random.split(key, 64))
    nrm = lambda shape, s: s * jax.random.normal(next(ks), shape, jnp.float32)
    uni = lambda shape, lo, hi: jax.random.uniform(next(ks), shape, jnp.float32, lo, hi)
    L, D, W = DEPTH, D_MODEL, BRANCH_WIDTH
    G, P, E, F = SSM_GROUPS, SSM_STATE, N_EXPERTS, D_EXPERT
    n_idx = jnp.arange(P, dtype=jnp.float32)
    return {
        'x': nrm((BATCH, SEQ, D), 1.0),
        'c': nrm((BATCH, D), 1.0),
        'ctx': nrm((BATCH, CTX_LEN, D), 1.0),
        'c_ctx': nrm((D,), 1.0),
        'ada_w': nrm((L, D, 6 * D), 0.2 * D ** -0.5),
        'ada_b': nrm((L, 6 * D), 0.02),
        'norm_mix': 1.0 + nrm((L, D), 0.02),
        'norm_ffn': 1.0 + nrm((L, D), 0.02),
        'w_in': nrm((L, D, D_IN), D ** -0.5),
        'b_merge': nrm((L, N_BRANCH * D), 0.1),
        'qk_norm_q': 1.0 + nrm((L, HEAD_DIM), 0.02),
        'qk_norm_k': 1.0 + nrm((L, HEAD_DIM), 0.02),
        'att_sink': nrm((L, ATT_HEADS), 0.5),
        'ssm_lam_re': -0.5 + nrm((L, 2, G, P), 0.01),
        'ssm_lam_im': math.pi * n_idx + nrm((L, 2, G, P), 0.01),
        'ssm_log_dt': uni((L, 2, G), math.log(1e-3), math.log(1e-1)),
        'ssm_b_re': nrm((L, G, P, SSM_GROUP), (2 * SSM_GROUP) ** -0.5),
        'ssm_b_im': nrm((L, G, P, SSM_GROUP), (2 * SSM_GROUP) ** -0.5),
        'ssm_c_re': nrm((L, G, SSM_GROUP, P), (2 * P) ** -0.5),
        'ssm_c_im': nrm((L, G, SSM_GROUP, P), (2 * P) ** -0.5),
        'ssm_d': nrm((L, W), 1.0),
        'ssm_glu_w': nrm((L, W, 2 * W), W ** -0.5),
        'ssm_glu_b': nrm((L, 2 * W), 0.02),
        'rwkv_mu_prev': uni((L, sum(RWKV_COLS)), 0.0, 0.5),
        'rwkv_mu_next': uni((L, sum(RWKV_COLS)), 0.0, 0.5),
        'rwkv_w0': uni((L, 2, W), -6.0, -1.0),
        'rwkv_w2': nrm((L, 2, DECAY_LORA, W), 0.1 * DECAY_LORA ** -0.5),
        'rwkv_a0': nrm((L, 2, W), 0.1),
        'rwkv_a2': nrm((L, 2, ICLR_LORA, W), 0.1 * ICLR_LORA ** -0.5),
        'rwkv_g2': nrm((L, GATE_LORA, W), GATE_LORA ** -0.5),
        'rwkv_k_k': 0.85 + nrm((L, W), 0.02),
        'rwkv_k_a': 1.0 + nrm((L, W), 0.02),
        'rwkv_r_k': nrm((L, RWKV_HEADS, HEAD_DIM), 0.1),
        'rwkv_lnx_w': 1.0 + nrm((L, W), 0.02),
        'rwkv_lnx_b': nrm((L, W), 0.02),
        'hgrn_lb_raw': nrm((L, W), 0.5),
        'hgrn_norm': 1.0 + nrm((L, HEAD_DIM), 0.02),
        'w_branch': nrm((L, N_BRANCH, W, D), W ** -0.5),
        'w_out': nrm((L, D, D), D ** -0.5),
        'router_w': nrm((L, D, E), D ** -0.5),
        'router_b': nrm((L, E), 0.01),
        'exp_gu_w': nrm((L, E, D, 2 * F), D ** -0.5),
        'exp_gu_b': nrm((L, E, 2 * F), 0.02),
        'exp_down_w': nrm((L, E, F, D), F ** -0.5),
        'exp_down_b': nrm((L, E, D), 0.02),
    }


def reference(x, c, ctx, c_ctx, ada_w, ada_b, norm_mix, norm_ffn, w_in, b_merge, qk_norm_q, qk_norm_k, att_sink,
              ssm_lam_re, ssm_lam_im, ssm_log_dt, ssm_b_re, ssm_b_im, ssm_c_re, ssm_c_im, ssm_d, ssm_glu_w, ssm_glu_b,
              rwkv_mu_prev, rwkv_mu_next, rwkv_w0, rwkv_w2, rwkv_a0, rwkv_a2, rwkv_g2, rwkv_k_k, rwkv_k_a, rwkv_r_k,
              rwkv_lnx_w, rwkv_lnx_b, hgrn_lb_raw, hgrn_norm, w_branch, w_out, router_w, router_b,
              exp_gu_w, exp_gu_b, exp_down_w, exp_down_b):
    B, S, D = x.shape
    C = ctx.shape[1]
    rows = S // GRID_W
    cos, sin = axial_rope(rows)
    lb_all = jnp.cumsum(jax.nn.softmax(hgrn_lb_raw.astype(jnp.float32), axis=0), axis=0)
    lb_all = lb_all - lb_all[:1]
    h_ctx = ctx
    for l in range(DEPTH):
        need_ctx = l < DEPTH - 1
        m_lat = [t[:, None, :] for t in jnp.split(jax.nn.silu(c) @ ada_w[l] + ada_b[l], 6, axis=-1)]
        m_ctx = jnp.split(jax.nn.silu(c_ctx) @ ada_w[l] + ada_b[l], 6, axis=-1)
        hl = modulate(rms_norm(x, norm_mix[l]), m_lat[0], m_lat[1])
        hc = modulate(rms_norm(h_ctx, norm_mix[l]), m_ctx[0], m_ctx[1])
        ql_, kl_, vl_, ul, rwl, hgl, gtl = split_cols(hl @ w_in[l], IN_COLS)
        qc_, kc_, vc_, uc, rwc, hgc, gtc = split_cols(hc @ w_in[l], IN_COLS)
        att_c, att_l = attention_branch(qc_, kc_, vc_, ql_, kl_, vl_, qk_norm_q[l], qk_norm_k[l], att_sink[l],
                                        cos, sin, need_ctx)
        ssm_c, ssm_l = s5_branch(uc, ul, ssm_lam_re[l], ssm_lam_im[l], ssm_log_dt[l], ssm_b_re[l], ssm_b_im[l],
                                 ssm_c_re[l], ssm_c_im[l], ssm_d[l], ssm_glu_w[l], ssm_glu_b[l], need_ctx)
        rw_c, rw_l = rwkv_branch(rwc, rwl, rwkv_mu_prev[l], rwkv_mu_next[l], rwkv_w0[l], rwkv_w2[l], rwkv_a0[l],
                                 rwkv_a2[l], rwkv_g2[l], rwkv_k_k[l], rwkv_k_a[l], rwkv_r_k[l], rwkv_lnx_w[l],
                                 rwkv_lnx_b[l], need_ctx)
        hg_c, hg_l = hgrn_branch(hgc, hgl, lb_all[l], hgrn_norm[l], need_ctx)
        x = x + m_lat[2] * gated_merge(gtl, (att_l, ssm_l, rw_l, hg_l), b_merge[l], w_branch[l], w_out[l])
        hl2 = modulate(rms_norm(x, norm_ffn[l]), m_lat[3], m_lat[4])
        moe_args = (router_w[l], router_b[l], exp_gu_w[l], exp_gu_b[l], exp_down_w[l], exp_down_b[l])
        if need_ctx:
            h_ctx = h_ctx + m_ctx[2] * gated_merge(gtc, (att_c, ssm_c, rw_c, hg_c), b_merge[l], w_branch[l], w_out[l])
            hc2 = modulate(rms_norm(h_ctx, norm_ffn[l]), m_ctx[3], m_ctx[4])
            tokens = jnp.concatenate([hc2.reshape(B * C, D), hl2.reshape(B * S, D)], axis=0)
            y = moe_ffn(tokens, *moe_args)
            h_ctx = h_ctx + m_ctx[5] * y[:B * C].reshape(B, C, D)
            x = x + m_lat[5] * y[B * C:].reshape(B, S, D)
        else:
            y = moe_ffn(hl2.reshape(B * S, D), *moe_args)
            x = x + m_lat[5] * y.reshape(B, S, D)
    return x
```

```python
import functools
import math
import jax
import jax.numpy as jnp
from jax import lax
import numpy as np
from jax.experimental import pallas as pl
from jax.experimental.pallas import tpu as pltpu

D_MODEL = 2048
BATCH = 1
SEQ = 8192
DEPTH = 2

GRID_W = 64
CTX_LEN = 256
HEAD_DIM = 64
N_BRANCH = 4
BRANCH_WIDTH = D_MODEL // 4

ATT_HEADS = BRANCH_WIDTH // HEAD_DIM
ATT_KV_HEADS = ATT_HEADS // 4
ATT_GROUP = ATT_HEADS // ATT_KV_HEADS
WINDOW = 128
ATT_BLOCK = 128
ROPE_BASE = 10000.0

SSM_GROUP = 16
SSM_GROUPS = BRANCH_WIDTH // SSM_GROUP
SSM_STATE = 64

RWKV_HEADS = BRANCH_WIDTH // HEAD_DIM
DECAY_LORA = 64
ICLR_LORA = 64
GATE_LORA = 128
RWKV_GN_EPS = 64e-5

HGRN_HEADS = BRANCH_WIDTH // HEAD_DIM
HGRN_CHUNK = 64

N_EXPERTS = 32
TOP_K = 4
D_EXPERT = D_MODEL // 2
SWIGLU_LIMIT = 7.0
SWIGLU_ALPHA = 1.702
MOE_BLOCK = 128

NORM_EPS = 1e-6

ATT_COLS = (ATT_HEADS * HEAD_DIM, ATT_KV_HEADS * HEAD_DIM, ATT_KV_HEADS * HEAD_DIM)
RWKV_COLS = (BRANCH_WIDTH, BRANCH_WIDTH, BRANCH_WIDTH, DECAY_LORA, DECAY_LORA, ICLR_LORA, ICLR_LORA, GATE_LORA)
HGRN_COLS = (BRANCH_WIDTH,) * 5
IN_COLS = ATT_COLS + (BRANCH_WIDTH, sum(RWKV_COLS), sum(HGRN_COLS), N_BRANCH * D_MODEL)
D_IN = sum(IN_COLS)

VMEM_LIMIT_BYTES = 48 * 1024 * 1024


def split_cols(p, sizes):
    idx = [sum(sizes[:i + 1]) for i in range(len(sizes) - 1)]
    return jnp.split(p, idx, axis=-1)


def rms_norm(x, w):
    xf = x.astype(jnp.float32)
    y = xf * lax.rsqrt(jnp.mean(xf * xf, axis=-1, keepdims=True) + NORM_EPS)
    return (y * w).astype(x.dtype)


def modulate(h, shift, scale):
    return h * (1.0 + scale) + shift


def axial_rope(rows):
    n_freq = HEAD_DIM // 4
    inv = ROPE_BASE ** (-jnp.arange(n_freq, dtype=jnp.float32) / n_freq)
    row = jnp.repeat(jnp.arange(rows, dtype=jnp.float32), GRID_W)
    col = jnp.tile(jnp.arange(GRID_W, dtype=jnp.float32), rows)
    ang = jnp.concatenate([row[:, None] * inv, col[:, None] * inv], axis=-1)
    return jnp.cos(ang), jnp.sin(ang)


def apply_rope(x, cos, sin):
    xf = x.astype(jnp.float32)
    half = HEAD_DIM // 2
    x1, x2 = xf[..., :half], xf[..., half:]
    cs, sn = cos[None, :, None, :], sin[None, :, None, :]
    return jnp.concatenate([x1 * cs - x2 * sn, x1 * sn + x2 * cs], axis=-1).astype(x.dtype)


def softmax_with_sink(logits, sink):
    m = jnp.maximum(jnp.max(logits, axis=-1, keepdims=True), sink)
    e = jnp.exp(logits - m)
    return e / (jnp.sum(e, axis=-1, keepdims=True) + jnp.exp(sink - m))


def attention_branch(q_c, k_c, v_c, q_l, k_l, v_l, qn_w, kn_w, sink, cos, sin, need_ctx):
    B, S = q_l.shape[:2]
    C = q_c.shape[1]
    heads = lambda t, h: t.reshape(t.shape[0], t.shape[1], h, HEAD_DIM)
    kc = rms_norm(heads(k_c, ATT_KV_HEADS), kn_w)
    vc = heads(v_c, ATT_KV_HEADS)
    ql = apply_rope(rms_norm(heads(q_l, ATT_HEADS), qn_w), cos, sin)
    kl = apply_rope(rms_norm(heads(k_l, ATT_KV_HEADS), kn_w), cos, sin)
    vl = heads(v_l, ATT_KV_HEADS)
    scale = HEAD_DIM ** -0.5
    sink_f = sink.astype(jnp.float32).reshape(ATT_KV_HEADS, ATT_GROUP)
    nb = S // ATT_BLOCK
    qb = ql.reshape(B, nb, ATT_BLOCK, ATT_KV_HEADS, ATT_GROUP, HEAD_DIM)

    def band(t):
        tp = jnp.pad(t, ((0, 0), (ATT_BLOCK, ATT_BLOCK), (0, 0), (0, 0)))
        tp = tp.reshape(B, nb + 2, ATT_BLOCK, ATT_KV_HEADS, HEAD_DIM)
        return jnp.concatenate([tp[:, :-2], tp[:, 1:-1], tp[:, 2:]], axis=2)

    kb, vb = band(kl), band(vl)
    s_win = jnp.einsum('bnqhgd,bnkhd->bnhgqk', qb, kb, preferred_element_type=jnp.float32) * scale
    q_pos = jnp.arange(nb)[:, None] * ATT_BLOCK + jnp.arange(ATT_BLOCK)[None, :]
    k_pos = jnp.arange(nb)[:, None] * ATT_BLOCK + jnp.arange(3 * ATT_BLOCK)[None, :] - ATT_BLOCK
    kp = k_pos[:, None, :]
    valid = (jnp.abs(kp - q_pos[:, :, None]) <= WINDOW) & (kp >= 0) & (kp < S)
    s_win = jnp.where(valid[None, :, None, None], s_win, -jnp.inf)
    s_ctx = jnp.einsum('bnqhgd,bkhd->bnhgqk', qb, kc, preferred_element_type=jnp.float32) * scale
    p = softmax_with_sink(jnp.concatenate([s_win, s_ctx], axis=-1), sink_f[None, None, :, :, None, None])
    p = p.astype(vl.dtype)
    o = (jnp.einsum('bnhgqk,bnkhd->bnqhgd', p[..., :3 * ATT_BLOCK], vb)
         + jnp.einsum('bnhgqk,bkhd->bnqhgd', p[..., 3 * ATT_BLOCK:], vc))
    lat = o.reshape(B, S, ATT_HEADS * HEAD_DIM)
    ctx_out = None
    if need_ctx:
        qc = rms_norm(heads(q_c, ATT_HEADS), qn_w).reshape(B, C, ATT_KV_HEADS, ATT_GROUP, HEAD_DIM)
        s_cc = jnp.einsum('bqhgd,bkhd->bhgqk', qc, kc, preferred_element_type=jnp.float32) * scale
        pc = softmax_with_sink(s_cc, sink_f[None, :, :, None, None]).astype(vc.dtype)
        ctx_out = jnp.einsum('bhgqk,bkhd->bqhgd', pc, vc).reshape(B, C, ATT_HEADS * HEAD_DIM)
    return ctx_out, lat


def cmul(ar, ai, br, bi):
    return ar * br - ai * bi, ar * bi + ai * br


def diag_scan(a_re, a_im, b_re, b_im, h_re, h_im, reverse):
    a_re = jnp.broadcast_to(a_re, b_re.shape)
    a_im = jnp.broadcast_to(a_im, b_re.shape)

    def combine(e1, e2):
        a1r, a1i, x1r, x1i = e1
        a2r, a2i, x2r, x2i = e2
        ar, ai = cmul(a2r, a2i, a1r, a1i)
        xr, xi = cmul(a2r, a2i, x1r, x1i)
        return ar, ai, xr + x2r, xi + x2i

    pr, pi_, xr, xi = lax.associative_scan(combine, (a_re, a_im, b_re, b_im), reverse=reverse, axis=1)
    hr, hi = cmul(pr, pi_, h_re[:, None], h_im[:, None])
    return xr + hr, xi + hi


S5_BLOCK = 128


def _s5_scan_kernel(u_ref, bre_ref, bim_ref, pwr_ref, pwi_ref, cre_ref, cim_ref, y_ref, car_ref):
    TB = S5_BLOCK
    f32 = jnp.float32

    @pl.when(pl.program_id(1) == 0)
    def _():
        car_ref[...] = jnp.zeros_like(car_ref)

    u = u_ref[0].astype(jnp.bfloat16)
    xr = jnp.dot(u, bre_ref[0], preferred_element_type=f32)
    xi = jnp.dot(u, bim_ref[0], preferred_element_type=f32)
    rows = lax.broadcasted_iota(jnp.int32, (TB, 1), 0)
    s = 1
    while s < TB:
        ar = pwr_ref[0, s - 1:s, :]
        ai = pwi_ref[0, s - 1:s, :]
        keep = rows >= s
        sr = jnp.where(keep, pltpu.roll(xr, s, axis=0), 0.0)
        si = jnp.where(keep, pltpu.roll(xi, s, axis=0), 0.0)
        xr, xi = xr + (ar * sr - ai * si), xi + (ar * si + ai * sr)
        s *= 2
    cr = car_ref[0:1, :]
    ci = car_ref[1:2, :]
    pr = pwr_ref[0]
    pi_ = pwi_ref[0]
    xr = xr + (pr * cr - pi_ * ci)
    xi = xi + (pr * ci + pi_ * cr)
    car_ref[0:1, :] = xr[TB - 1:TB, :]
    car_ref[1:2, :] = xi[TB - 1:TB, :]
    y_ref[0] = (jnp.dot(xr.astype(jnp.bfloat16), cre_ref[...], preferred_element_type=f32)
                - jnp.dot(xi.astype(jnp.bfloat16), cim_ref[...], preferred_element_type=f32))


def s5_scan_pallas(u, bt_re, bt_im, pw_re, pw_im, c_re, c_im):
    _, T, W = u.shape
    N = bt_re.shape[-1]
    TB = S5_BLOCK
    return pl.pallas_call(
        _s5_scan_kernel,
        grid=(2, T // TB),
        in_specs=[pl.BlockSpec((1, TB, W), lambda d, i: (d, i, 0)),
                  pl.BlockSpec((1, W, N), lambda d, i: (d, 0, 0)),
                  pl.BlockSpec((1, W, N), lambda d, i: (d, 0, 0)),
                  pl.BlockSpec((1, TB, N), lambda d, i: (d, 0, 0)),
                  pl.BlockSpec((1, TB, N), lambda d, i: (d, 0, 0)),
                  pl.BlockSpec((N, W), lambda d, i: (0, 0)),
                  pl.BlockSpec((N, W), lambda d, i: (0, 0))],
        out_specs=pl.BlockSpec((1, TB, W), lambda d, i: (d, i, 0)),
        out_shape=jax.ShapeDtypeStruct((2, T, W), jnp.float32),
        scratch_shapes=[pltpu.VMEM((2, N), jnp.float32)],
        compiler_params=pltpu.CompilerParams(dimension_semantics=("parallel", "arbitrary"),
                                             vmem_limit_bytes=VMEM_LIMIT_BYTES),
    )(u, bt_re, bt_im, pw_re, pw_im, c_re, c_im)


def s5_branch(u_c, u_l, lam_re, lam_im, log_dt, b_re, b_im, c_re, c_im, d_skip, glu_w, glu_b, need_ctx):
    f32 = jnp.float32
    assert u_l.shape[0] == 1
    C = u_c.shape[1]
    G, P, GC = SSM_GROUPS, SSM_STATE, SSM_GROUP
    dt = jnp.exp(log_dt.astype(f32))[..., None]
    lr, li = lam_re.astype(f32), lam_im.astype(f32)
    mag = jnp.exp(dt * lr)
    ab_re, ab_im = mag * jnp.cos(dt * li), mag * jnp.sin(dt * li)
    den = lr * lr + li * li
    gm_re = ((ab_re - 1.0) * lr + ab_im * li) / den
    gm_im = (ab_im * lr - (ab_re - 1.0) * li) / den
    eye_g = jnp.eye(G, dtype=f32)
    br = jnp.transpose(b_re.astype(f32), (0, 2, 1))
    bi = jnp.transpose(b_im.astype(f32), (0, 2, 1))
    gr, gi = gm_re[:, :, None, :], gm_im[:, :, None, :]
    blockdiag = lambda m: jnp.einsum('dgcp,gh->dgchp', m, eye_g).reshape(2, G * GC, G * P)
    bt_re = blockdiag(gr * br - gi * bi).astype(jnp.bfloat16)
    bt_im = blockdiag(gr * bi + gi * br).astype(jnp.bfloat16)
    n = jnp.arange(1, S5_BLOCK + 1, dtype=f32)[None, :, None, None]
    theta = dt * li
    theta = theta - (2.0 * math.pi) * jnp.round(theta / (2.0 * math.pi))
    pmag = jnp.exp(n * (dt * lr)[:, None])
    pw_re = (pmag * jnp.cos(n * theta[:, None])).reshape(2, S5_BLOCK, G * P)
    pw_im = (pmag * jnp.sin(n * theta[:, None])).reshape(2, S5_BLOCK, G * P)
    cbd = lambda m: jnp.einsum('gcp,gh->gphc', m.astype(f32), eye_g).reshape(G * P, G * GC).astype(jnp.bfloat16)
    uc, ul = u_c[0].astype(f32), u_l[0].astype(f32)
    u2 = jnp.stack([jnp.concatenate([uc, ul], axis=0), jnp.concatenate([uc[::-1], ul[::-1]], axis=0)])
    yp = s5_scan_pallas(u2, bt_re, bt_im, pw_re, pw_im, cbd(c_re), cbd(c_im))

    def readout(y, u):
        y = y + d_skip * u.astype(f32)
        z = jax.nn.gelu(y) @ glu_w + glu_b
        za, zb = jnp.split(z, 2, axis=-1)
        return (za * jax.nn.sigmoid(zb)).astype(u.dtype)

    lat = readout((yp[0, C:] + yp[1, C:][::-1])[None], u_l)
    ctx_out = readout((yp[0, :C] + yp[1, :C][::-1])[None], u_c) if need_ctx else None
    return ctx_out, lat


def shift_mix(p, mu_prev, mu_next):
    zero = jnp.zeros_like(p[:, :1])
    prev = jnp.concatenate([zero, p[:, :-1]], axis=1)
    nxt = jnp.concatenate([p[:, 1:], zero], axis=1)
    return p + mu_prev * (prev - p) + mu_next * (nxt - p)


def rwkv_scan(r, w, k, v, kk, a, s0, reverse):
    def step(S, inp):
        r_t, w_t, k_t, v_t, kk_t, a_t = inp
        sa = jnp.einsum('bhvk,bhk->bhv', S, -kk_t, precision=lax.Precision.HIGHEST)
        S = (S * w_t[:, :, None, :] + sa[..., None] * (kk_t * a_t)[:, :, None, :]
             + v_t[..., None] * k_t[:, :, None, :])
        return S, jnp.einsum('bhvk,bhk->bhv', S, r_t, precision=lax.Precision.HIGHEST)

    xs = tuple(jnp.moveaxis(t, 1, 0) for t in (r, w, k, v, kk, a))
    S, ys = lax.scan(step, s0, xs, reverse=reverse)
    return S, jnp.moveaxis(ys, 0, 1)


RWKV_CHUNK = 64
RWKV_INV_BLOCK = 16
RWKV_PREP_UNROLL = 4

_NN = (((1,), (0,)), ((), ()))
_NT = (((1,), (1,)), ((), ()))
_TN = (((0,), (0,)), ((), ()))


def _split_bf16(x):
    hi = x.astype(jnp.bfloat16)
    lo = (x - hi.astype(jnp.float32)).astype(jnp.bfloat16)
    return hi, lo


def _dot3(a, b, dims=_NN):
    ah, al = _split_bf16(a)
    bh, bl = _split_bf16(b)
    d = functools.partial(lax.dot_general, dimension_numbers=dims, preferred_element_type=jnp.float32)
    return d(ah, bh) + (d(ah, bl) + d(al, bh))


def _rwkv_chunk_maps(lw, r, kk, be, kd, v):
    L = RWKV_CHUNK
    f32 = jnp.float32
    row = lax.broadcasted_iota(jnp.int32, (L, L), 0)
    col = lax.broadcasted_iota(jnp.int32, (L, L), 1)
    tril_i = row >= col
    tril_s = row > col
    blk = (row // RWKV_INV_BLOCK) == (col // RWKV_INV_BLOCK)
    eye = jnp.where(row == col, 1.0, 0.0).astype(f32)
    tri = jnp.where(tril_i, 1.0, 0.0).astype(jnp.bfloat16)
    h1 = lw.astype(jnp.bfloat16)
    r1 = lw - h1.astype(f32)
    h2 = r1.astype(jnp.bfloat16)
    h3 = (r1 - h2.astype(f32)).astype(jnp.bfloat16)
    dd = functools.partial(jnp.dot, preferred_element_type=f32)
    c = dd(tri, h1) + (dd(tri, h2) + dd(tri, h3))
    c_end = c[L - 1:L, :]
    e_c = jnp.exp(c)
    e_nc = jnp.exp(-c)
    e_end = jnp.exp(c_end)
    al = kk * jnp.exp(c - lw)
    rt = r * e_c
    bb = be * e_nc
    kb = kd * e_nc
    bh = bb * e_end
    kh = kb * e_end
    lhs = jnp.concatenate([al, rt], axis=0)
    g1 = _dot3(lhs, bb, _NT)
    g2 = _dot3(lhs, kb, _NT)
    n = jnp.where(tril_s, g1[:L], 0.0)
    arb = jnp.where(tril_i, g1[L:], 0.0)
    aak = jnp.where(tril_s, g2[:L], 0.0)
    ark = jnp.where(tril_i, g2[L:], 0.0)
    d1 = jnp.where(blk, n, 0.0)
    o1 = n - d1
    d2 = _dot3(d1, d1)
    d4 = _dot3(d2, d2)
    d8 = _dot3(d4, d4)
    imd = eye - d1
    t1 = imd + _dot3(imd, d2)
    t2 = t1 + _dot3(t1, d4)
    dinv = t2 + _dot3(t2, d8)
    e1 = _dot3(dinv, o1)
    e2 = _dot3(e1, e1)
    ime = eye - e1
    fm = ime + _dot3(ime, e2)
    minv = _dot3(fm, dinv)
    at = _dot3(minv, al)
    pm = _dot3(minv, _dot3(aak, v))
    rh = rt - _dot3(arb, at)
    y0 = _dot3(ark, v) - _dot3(arb, pm)
    gt = eye * e_end - _dot3(bh, at, _TN)
    ht = _dot3(kh, v, _TN) - _dot3(bh, pm, _TN)
    return gt, ht, rh, y0


def _rwkv_prep_kernel(lw_ref, r_ref, kk_ref, be_ref, kd_ref, v_ref, gt_ref, ht_ref, rh_ref, y0_ref):
    def heads(i, carry):
        for j in range(RWKV_PREP_UNROLL):
            h = i * RWKV_PREP_UNROLL + j
            gt, ht, rh, y0 = _rwkv_chunk_maps(lw_ref[0, h], r_ref[0, h], kk_ref[0, h], be_ref[0, h],
                                              kd_ref[0, h], v_ref[0, h])
            gt_ref[0, h, 0] = gt
            ht_ref[0, h, 0] = ht
            rh_ref[0, h, 0] = rh
            y0_ref[0, h, 0] = y0
        return carry

    lax.fori_loop(0, RWKV_HEADS // RWKV_PREP_UNROLL, heads, 0)


def _rwkv_scan_kernel(gt_ref, ht_ref, rh_ref, y0_ref, y_ref, st_ref):
    @pl.when(pl.program_id(0) == 0)
    def _():
        st_ref[...] = jnp.zeros_like(st_ref)

    for d in range(2):
        for h in range(RWKV_HEADS):
            st = st_ref[d, h]
            y_ref[d, h] = _dot3(rh_ref[d, h, 0], st) + y0_ref[d, h, 0]
            st_ref[d, h] = _dot3(gt_ref[d, h, 0], st) + ht_ref[d, h, 0]


def rwkv_scan_pallas(lw, r, kk, be, kd, v):
    _, H, T, N = lw.shape
    L = RWKV_CHUNK
    nc = T // L
    in_spec = pl.BlockSpec((1, H, L, N), lambda d, c: (d, 0, c, 0))
    map_spec = pl.BlockSpec((1, H, 1, N, N), lambda d, c: (d, 0, c, 0, 0))
    map_shape = jax.ShapeDtypeStruct((2, H, nc, N, N), jnp.float32)
    gt, ht, rh, y0 = pl.pallas_call(
        _rwkv_prep_kernel,
        grid=(2, nc),
        in_specs=[in_spec] * 6,
        out_specs=[map_spec] * 4,
        out_shape=[map_shape] * 4,
        compiler_params=pltpu.CompilerParams(dimension_semantics=("parallel", "parallel"),
                                             vmem_limit_bytes=VMEM_LIMIT_BYTES),
    )(lw, r, kk, be, kd, v)
    scan_spec = pl.BlockSpec((2, H, 1, N, N), lambda c: (0, 0, c, 0, 0))
    return pl.pallas_call(
        _rwkv_scan_kernel,
        grid=(nc,),
        in_specs=[scan_spec] * 4,
        out_specs=pl.BlockSpec((2, H, L, N), lambda c: (0, 0, c, 0)),
        out_shape=jax.ShapeDtypeStruct((2, H, T, N), jnp.float32),
        scratch_shapes=[pltpu.VMEM((2, H, N, N), jnp.float32)],
        compiler_params=pltpu.CompilerParams(dimension_semantics=("arbitrary",),
                                             vmem_limit_bytes=VMEM_LIMIT_BYTES),
    )(gt, ht, rh, y0)


def rwkv_branch(p_c, p_l, mu_prev, mu_next, w0, w2, a0, a2, g2, k_k, k_a, r_k, lnx_w, lnx_b, need_ctx):
    f32 = jnp.float32
    W = BRANCH_WIDTH
    assert p_l.shape[0] == 1
    C = p_c.shape[1]

    def features(p):
        B, T = p.shape[:2]
        hd = lambda t: t.reshape(B, T, RWKV_HEADS, HEAD_DIM)
        pm = shift_mix(p.astype(f32), mu_prev, mu_next)
        r, k, v, w1f, w1b, a1f, a1b, g1 = split_cols(pm, RWKV_COLS)
        kk = hd(k * k_k)
        kk = kk * lax.rsqrt(jnp.maximum(jnp.sum(kk * kk, axis=-1, keepdims=True), 1e-12))
        dirs = []
        for d, (w1, a1) in enumerate(((w1f, a1f), (w1b, a1b))):
            w = -jax.nn.softplus(-(w0[d] + jnp.tanh(w1) @ w2[d])) - 0.5
            log_decay = -jnp.exp(w)
            a = jax.nn.sigmoid(a0[d] + a1 @ a2[d])
            k_d = k * (1.0 + (a - 1.0) * k_a)
            dirs.append((hd(log_decay), hd(k_d), hd(a)))
        g = jax.nn.sigmoid(g1) @ g2
        return hd(r), hd(v), kk, dirs, g

    def readout(y, r, v, dirs, g, dtype):
        B, T = y.shape[:2]
        mu = jnp.mean(y, axis=-1, keepdims=True)
        var = jnp.mean(jnp.square(y - mu), axis=-1, keepdims=True)
        yn = ((y - mu) * lax.rsqrt(var + RWKV_GN_EPS)).reshape(B, T, W) * lnx_w + lnx_b
        bonus = jnp.sum(r * (dirs[0][1] + dirs[1][1]) * r_k, axis=-1, keepdims=True) * v
        return ((yn + bonus.reshape(B, T, W)) * g).astype(dtype)

    rc, vc, kkc, dirs_c, gc = features(p_c)
    rl, vl, kkl, dirs_l, gl = features(p_l)

    def order(tc, tl):
        fwd = jnp.concatenate([tc[0], tl[0]], axis=0)
        bwd = jnp.concatenate([tc[0, ::-1], tl[0, ::-1]], axis=0)
        return jnp.stack([fwd, bwd]).transpose(0, 2, 1, 3)

    def order2(tc0, tl0, tc1, tl1):
        fwd = jnp.concatenate([tc0[0], tl0[0]], axis=0)
        bwd = jnp.concatenate([tc1[0, ::-1], tl1[0, ::-1]], axis=0)
        return jnp.stack([fwd, bwd]).transpose(0, 2, 1, 3)

    lw = order2(dirs_c[0][0], dirs_l[0][0], dirs_c[1][0], dirs_l[1][0])
    kd = order2(dirs_c[0][1], dirs_l[0][1], dirs_c[1][1], dirs_l[1][1])
    be = order2(kkc * dirs_c[0][2], kkl * dirs_l[0][2], kkc * dirs_c[1][2], kkl * dirs_l[1][2])
    y = rwkv_scan_pallas(lw, order(rc, rl), order(kkc, kkl), be, kd, order(vc, vl))
    y = y.transpose(0, 2, 1, 3)
    y_c = (y[0, :C] + y[1, :C][::-1])[None]
    y_l = (y[0, C:] + y[1, C:][::-1])[None]
    lat = readout(y_l, rl, vl, dirs_l, gl, p_l.dtype)
    ctx_out = readout(y_c, rc, vc, dirs_c, gc, p_c.dtype) if need_ctx else None
    return ctx_out, lat


def hgrn_chunk_scan(q, k, v, log_f, s0, reverse):
    if reverse:
        q, k, v, log_f = (jnp.flip(t, axis=1) for t in (q, k, v, log_f))
    B, T, H, _ = q.shape
    Dv = v.shape[-1]
    nc = T // HGRN_CHUNK
    chunk = lambda t: t.reshape(B, nc, HGRN_CHUNK, H, t.shape[-1]).transpose(1, 0, 3, 2, 4)
    causal = jnp.tril(jnp.ones((HGRN_CHUNK, HGRN_CHUNK), dtype=bool))

    def step(S, inp):
        qc, kc, vc, lf = inp
        b = jnp.cumsum(lf, axis=2)
        o = jnp.einsum('bhtd,bhdv->bhtv', qc * jnp.exp(b), S)
        diff = jnp.where(causal[:, :, None], b[:, :, :, None, :] - b[:, :, None, :, :], -jnp.inf)
        att = jnp.einsum('bhtd,bhsd,bhtsd->bhts', qc, kc, jnp.exp(diff))
        o = o + jnp.einsum('bhts,bhsv->bhtv', att, vc)
        b_end = b[:, :, -1:]
        S = (jnp.exp(b_end[:, :, 0])[..., None] * S
             + jnp.einsum('bhsd,bhsv->bhdv', kc * jnp.exp(b_end - b), vc))
        return S, o

    S, o = lax.scan(step, s0, (chunk(q), chunk(k), chunk(v), chunk(log_f)))
    o = o.transpose(1, 0, 3, 2, 4).reshape(B, T, H, Dv)
    if reverse:
        o = jnp.flip(o, axis=1)
    return S, o


def hgrn_branch(p_c, p_l, lb, norm_w, need_ctx):
    f32 = jnp.float32
    log_lb, log_ub = jnp.log(lb), jnp.log1p(-lb)

    def features(p):
        B, T = p.shape[:2]
        hd = lambda t: t.reshape(B, T, HGRN_HEADS, HEAD_DIM)
        q, zf, zb, i, g = split_cols(p.astype(f32), HGRN_COLS)
        gates = []
        for z in (zf, zb):
            log_f = jnp.logaddexp(log_lb, log_ub + jax.nn.log_sigmoid(z))
            k = (1.0 - lb) * jax.nn.sigmoid(-z)
            gates.append((hd(log_f), hd(k)))
        return hd(q), hd(i), gates, g

    def readout(o, g, dtype):
        B, T = o.shape[:2]
        return (rms_norm(o, norm_w).reshape(B, T, BRANCH_WIDTH) * jax.nn.silu(g)).astype(dtype)

    qc, ic, gates_c, gc = features(p_c)
    ql, il, gates_l, gl = features(p_l)
    B = p_l.shape[0]
    s0 = jnp.zeros((B, HGRN_HEADS, HEAD_DIM, HEAD_DIM), f32)
    oc, ol = [], []
    for d, reverse in enumerate((False, True)):
        sc, odc = hgrn_chunk_scan(qc, gates_c[d][1], ic, gates_c[d][0], s0, reverse)
        _, odl = hgrn_chunk_scan(ql, gates_l[d][1], il, gates_l[d][0], sc, reverse)
        oc.append(odc)
        ol.append(odl)
    lat = readout(ol[0] + ol[1], gl, p_l.dtype)
    ctx_out = readout(oc[0] + oc[1], gc, p_c.dtype) if need_ctx else None
    return ctx_out, lat


def gated_merge(gate_cols, branches, b_merge, w_branch, w_out):
    B, T = gate_cols.shape[:2]
    gates = jax.nn.sigmoid(gate_cols + b_merge).reshape(B, T, N_BRANCH, D_MODEL)
    br = jnp.stack(branches, axis=2)
    proj = jnp.einsum('btnc,ncd->btnd', br, w_branch)
    return jnp.sum(gates * proj, axis=2) @ w_out


def moe_ffn(h, router_w, router_b, gu_w, gu_b, dn_w, dn_b):
    N, D = h.shape
    logits = (h @ router_w + router_b).astype(jnp.float32)
    top_l, top_e = lax.top_k(logits, TOP_K)
    weights = jax.nn.softmax(top_l, axis=-1)
    n_assign = N * TOP_K
    flat_e = top_e.reshape(-1)
    order = jnp.argsort(flat_e)
    e_sorted = flat_e[order]
    tok_sorted = (jnp.arange(n_assign, dtype=jnp.int32) // TOP_K)[order]
    w_sorted = weights.reshape(-1)[order].astype(h.dtype)
    counts = jnp.bincount(flat_e, length=N_EXPERTS)
    padded = (counts + MOE_BLOCK - 1) // MOE_BLOCK * MOE_BLOCK
    start_sorted = jnp.cumsum(counts) - counts
    pad_end = jnp.cumsum(padded)
    start_pad = pad_end - padded
    dest = start_pad[e_sorted] + jnp.arange(n_assign, dtype=jnp.int32) - start_sorted[e_sorted]
    n_blocks = -(-(n_assign + N_EXPERTS * (MOE_BLOCK - 1)) // MOE_BLOCK)
    cap = n_blocks * MOE_BLOCK
    slot_tok = jnp.full((cap,), N, dtype=jnp.int32).at[dest].set(tok_sorted)
    slot_w = jnp.zeros((cap,), h.dtype).at[dest].set(w_sorted)
    block_e = jnp.minimum(jnp.searchsorted(pad_end, jnp.arange(n_blocks, dtype=jnp.int32) * MOE_BLOCK, side='right'),
                          N_EXPERTS - 1)
    h_pad = jnp.concatenate([h, jnp.zeros((1, D), h.dtype)], axis=0)
    xb = h_pad[slot_tok].reshape(n_blocks, MOE_BLOCK, D)

    def expert_block(args):
        xblk, e = args
        gu = xblk @ gu_w[e] + gu_b[e]
        gate, up = gu[:, ::2], gu[:, 1::2]
        gate = jnp.minimum(gate, SWIGLU_LIMIT)
        up = jnp.clip(up, -SWIGLU_LIMIT, SWIGLU_LIMIT)
        glu = gate * jax.nn.sigmoid(SWIGLU_ALPHA * gate)
        return ((up + 1.0) * glu) @ dn_w[e] + dn_b[e]

    yb = lax.map(expert_block, (xb, block_e)).reshape(cap, D)
    y = jnp.zeros((N + 1, D), h.dtype).at[slot_tok].add(yb * slot_w[:, None])
    return y[:N]


def _mm_kernel(a_ref, b_ref, o_ref):
    o_ref[...] = jnp.dot(a_ref[...].astype(jnp.bfloat16), b_ref[...].astype(jnp.bfloat16),
                         preferred_element_type=jnp.float32)


def _pick(n, cands):
    for c in cands:
        if n % c == 0:
            return c
    return n


def pallas_matmul(a, b):
    M, K = a.shape
    N = b.shape[1]
    tm = _pick(M, (768, 512, 384, 256, 128))
    tn = _pick(N, (512, 384, 256, 128))
    return pl.pallas_call(
        _mm_kernel,
        grid=(M // tm, N // tn),
        in_specs=[pl.BlockSpec((tm, K), lambda i, j: (i, 0)),
                  pl.BlockSpec((K, tn), lambda i, j: (0, j))],
        out_specs=pl.BlockSpec((tm, tn), lambda i, j: (i, j)),
        out_shape=jax.ShapeDtypeStruct((M, N), jnp.float32),
        compiler_params=pltpu.CompilerParams(dimension_semantics=("parallel", "parallel"),
                                             vmem_limit_bytes=VMEM_LIMIT_BYTES),
    )(a, b)


def kernel(x, c, ctx, c_ctx, ada_w, ada_b, norm_mix, norm_ffn, w_in, b_merge, qk_norm_q, qk_norm_k, att_sink,
           ssm_lam_re, ssm_lam_im, ssm_log_dt, ssm_b_re, ssm_b_im, ssm_c_re, ssm_c_im, ssm_d, ssm_glu_w, ssm_glu_b,
           rwkv_mu_prev, rwkv_mu_next, rwkv_w0, rwkv_w2, rwkv_a0, rwkv_a2, rwkv_g2, rwkv_k_k, rwkv_k_a, rwkv_r_k,
           rwkv_lnx_w, rwkv_lnx_b, hgrn_lb_raw, hgrn_norm, w_branch, w_out, router_w, router_b,
           exp_gu_w, exp_gu_b, exp_down_w, exp_down_b):
    B, S, D = x.shape
    C = ctx.shape[1]
    rows = S // GRID_W
    cos, sin = axial_rope(rows)
    lb_all = jnp.cumsum(jax.nn.softmax(hgrn_lb_raw.astype(jnp.float32), axis=0), axis=0)
    lb_all = lb_all - lb_all[:1]
    h_ctx = ctx
    for l in range(DEPTH):
        need_ctx = l < DEPTH - 1
        m_lat = [t[:, None, :] for t in jnp.split(jax.nn.silu(c) @ ada_w[l] + ada_b[l], 6, axis=-1)]
        m_ctx = jnp.split(jax.nn.silu(c_ctx) @ ada_w[l] + ada_b[l], 6, axis=-1)
        hl = modulate(rms_norm(x, norm_mix[l]), m_lat[0], m_lat[1])
        hc = modulate(rms_norm(h_ctx, norm_mix[l]), m_ctx[0], m_ctx[1])
        ql_, kl_, vl_, ul, rwl, hgl, gtl = split_cols(pallas_matmul(hl[0], w_in[l])[None], IN_COLS)
        qc_, kc_, vc_, uc, rwc, hgc, gtc = split_cols(pallas_matmul(hc[0], w_in[l])[None], IN_COLS)
        att_c, att_l = attention_branch(qc_, kc_, vc_, ql_, kl_, vl_, qk_norm_q[l], qk_norm_k[l], att_sink[l],
                                        cos, sin, need_ctx)
        ssm_c, ssm_l = s5_branch(uc, ul, ssm_lam_re[l], ssm_lam_im[l], ssm_log_dt[l], ssm_b_re[l], ssm_b_im[l],
                                 ssm_c_re[l], ssm_c_im[l], ssm_d[l], ssm_glu_w[l], ssm_glu_b[l], need_ctx)
        rw_c, rw_l = rwkv_branch(rwc, rwl, rwkv_mu_prev[l], rwkv_mu_next[l], rwkv_w0[l], rwkv_w2[l], rwkv_a0[l],
                                 rwkv_a2[l], rwkv_g2[l], rwkv_k_k[l], rwkv_k_a[l], rwkv_r_k[l], rwkv_lnx_w[l],
                                 rwkv_lnx_b[l], need_ctx)
        hg_c, hg_l = hgrn_branch(hgc, hgl, lb_all[l], hgrn_norm[l], need_ctx)
        x = x + m_lat[2] * gated_merge(gtl, (att_l, ssm_l, rw_l, hg_l), b_merge[l], w_branch[l], w_out[l])
        hl2 = modulate(rms_norm(x, norm_ffn[l]), m_lat[3], m_lat[4])
        moe_args = (router_w[l], router_b[l], exp_gu_w[l], exp_gu_b[l], exp_down_w[l], exp_down_b[l])
        if need_ctx:
            h_ctx = h_ctx + m_ctx[2] * gated_merge(gtc, (att_c, ssm_c, rw_c, hg_c), b_merge[l], w_branch[l], w_out[l])
            hc2 = modulate(rms_norm(h_ctx, norm_ffn[l]), m_ctx[3], m_ctx[4])
            tokens = jnp.concatenate([hc2.reshape(B * C, D), hl2.reshape(B * S, D)], axis=0)
            y = moe_ffn(tokens, *moe_args)
            h_ctx = h_ctx + m_ctx[5] * y[:B * C].reshape(B, C, D)
            x = x + m_lat[5] * y[B * C:].reshape(B, S, D)
        else:
            y = moe_ffn(hl2.reshape(B * S, D), *moe_args)
            x = x + m_lat[5] * y.reshape(B, S, D)
    return x
```

```python
import functools
import math
import jax
import jax.numpy as jnp
from jax import lax
import numpy as np
from jax.experimental import pallas as pl
from jax.experimental.pallas import tpu as pltpu

D_MODEL = 2048
BATCH = 1
SEQ = 8192
DEPTH = 2

GRID_W = 64
CTX_LEN = 256
HEAD_DIM = 64
N_BRANCH = 4
BRANCH_WIDTH = D_MODEL // 4

ATT_HEADS = BRANCH_WIDTH // HEAD_DIM
ATT_KV_HEADS = ATT_HEADS // 4
ATT_GROUP = ATT_HEADS // ATT_KV_HEADS
WINDOW = 128
ATT_BLOCK = 128
ROPE_BASE = 10000.0

SSM_GROUP = 16
SSM_GROUPS = BRANCH_WIDTH // SSM_GROUP
SSM_STATE = 64

RWKV_HEADS = BRANCH_WIDTH // HEAD_DIM
DECAY_LORA = 64
ICLR_LORA = 64
GATE_LORA = 128
RWKV_GN_EPS = 64e-5

HGRN_HEADS = BRANCH_WIDTH // HEAD_DIM
HGRN_CHUNK = 64

N_EXPERTS = 32
TOP_K = 4
D_EXPERT = D_MODEL // 2
SWIGLU_LIMIT = 7.0
SWIGLU_ALPHA = 1.702
MOE_BLOCK = 128

NORM_EPS = 1e-6

ATT_COLS = (ATT_HEADS * HEAD_DIM, ATT_KV_HEADS * HEAD_DIM, ATT_KV_HEADS * HEAD_DIM)
RWKV_COLS = (BRANCH_WIDTH, BRANCH_WIDTH, BRANCH_WIDTH, DECAY_LORA, DECAY_LORA, ICLR_LORA, ICLR_LORA, GATE_LORA)
HGRN_COLS = (BRANCH_WIDTH,) * 5
IN_COLS = ATT_COLS + (BRANCH_WIDTH, sum(RWKV_COLS), sum(HGRN_COLS), N_BRANCH * D_MODEL)
D_IN = sum(IN_COLS)

VMEM_LIMIT_BYTES = 48 * 1024 * 1024


def split_cols(p, sizes):
    idx = [sum(sizes[:i + 1]) for i in range(len(sizes) - 1)]
    return jnp.split(p, idx, axis=-1)


def rms_norm(x, w):
    xf = x.astype(jnp.float32)
    y = xf * lax.rsqrt(jnp.mean(xf * xf, axis=-1, keepdims=True) + NORM_EPS)
    return (y * w).astype(x.dtype)


def modulate(h, shift, scale):
    return h * (1.0 + scale) + shift


def axial_rope(rows):
    n_freq = HEAD_DIM // 4
    inv = ROPE_BASE ** (-jnp.arange(n_freq, dtype=jnp.float32) / n_freq)
    row = jnp.repeat(jnp.arange(rows, dtype=jnp.float32), GRID_W)
    col = jnp.tile(jnp.arange(GRID_W, dtype=jnp.float32), rows)
    ang = jnp.concatenate([row[:, None] * inv, col[:, None] * inv], axis=-1)
    return jnp.cos(ang), jnp.sin(ang)


def apply_rope(x, cos, sin):
    xf = x.astype(jnp.float32)
    half = HEAD_DIM // 2
    x1, x2 = xf[..., :half], xf[..., half:]
    cs, sn = cos[None, :, None, :], sin[None, :, None, :]
    return jnp.concatenate([x1 * cs - x2 * sn, x1 * sn + x2 * cs], axis=-1).astype(x.dtype)


def softmax_with_sink(logits, sink):
    m = jnp.maximum(jnp.max(logits, axis=-1, keepdims=True), sink)
    e = jnp.exp(logits - m)
    return e / (jnp.sum(e, axis=-1, keepdims=True) + jnp.exp(sink - m))


def attention_branch(q_c, k_c, v_c, q_l, k_l, v_l, qn_w, kn_w, sink, cos, sin, need_ctx):
    B, S = q_l.shape[:2]
    C = q_c.shape[1]
    heads = lambda t, h: t.reshape(t.shape[0], t.shape[1], h, HEAD_DIM)
    kc = rms_norm(heads(k_c, ATT_KV_HEADS), kn_w)
    vc = heads(v_c, ATT_KV_HEADS)
    ql = apply_rope(rms_norm(heads(q_l, ATT_HEADS), qn_w), cos, sin)
    kl = apply_rope(rms_norm(heads(k_l, ATT_KV_HEADS), kn_w), cos, sin)
    vl = heads(v_l, ATT_KV_HEADS)
    scale = HEAD_DIM ** -0.5
    sink_f = sink.astype(jnp.float32).reshape(ATT_KV_HEADS, ATT_GROUP)
    nb = S // ATT_BLOCK
    qb = ql.reshape(B, nb, ATT_BLOCK, ATT_KV_HEADS, ATT_GROUP, HEAD_DIM)

    def band(t):
        tp = jnp.pad(t, ((0, 0), (ATT_BLOCK, ATT_BLOCK), (0, 0), (0, 0)))
        tp = tp.reshape(B, nb + 2, ATT_BLOCK, ATT_KV_HEADS, HEAD_DIM)
        return jnp.concatenate([tp[:, :-2], tp[:, 1:-1], tp[:, 2:]], axis=2)

    kb, vb = band(kl), band(vl)
    s_win = jnp.einsum('bnqhgd,bnkhd->bnhgqk', qb, kb, preferred_element_type=jnp.float32) * scale
    q_pos = jnp.arange(nb)[:, None] * ATT_BLOCK + jnp.arange(ATT_BLOCK)[None, :]
    k_pos = jnp.arange(nb)[:, None] * ATT_BLOCK + jnp.arange(3 * ATT_BLOCK)[None, :] - ATT_BLOCK
    kp = k_pos[:, None, :]
    valid = (jnp.abs(kp - q_pos[:, :, None]) <= WINDOW) & (kp >= 0) & (kp < S)
    s_win = jnp.where(valid[None, :, None, None], s_win, -jnp.inf)
    s_ctx = jnp.einsum('bnqhgd,bkhd->bnhgqk', qb, kc, preferred_element_type=jnp.float32) * scale
    p = softmax_with_sink(jnp.concatenate([s_win, s_ctx], axis=-1), sink_f[None, None, :, :, None, None])
    p = p.astype(vl.dtype)
    o = (jnp.einsum('bnhgqk,bnkhd->bnqhgd', p[..., :3 * ATT_BLOCK], vb)
         + jnp.einsum('bnhgqk,bkhd->bnqhgd', p[..., 3 * ATT_BLOCK:], vc))
    lat = o.reshape(B, S, ATT_HEADS * HEAD_DIM)
    ctx_out = None
    if need_ctx:
        qc = rms_norm(heads(q_c, ATT_HEADS), qn_w).reshape(B, C, ATT_KV_HEADS, ATT_GROUP, HEAD_DIM)
        s_cc = jnp.einsum('bqhgd,bkhd->bhgqk', qc, kc, preferred_element_type=jnp.float32) * scale
        pc = softmax_with_sink(s_cc, sink_f[None, :, :, None, None]).astype(vc.dtype)
        ctx_out = jnp.einsum('bhgqk,bkhd->bqhgd', pc, vc).reshape(B, C, ATT_HEADS * HEAD_DIM)
    return ctx_out, lat


S5_BLOCK = 128


def _s5_scan_kernel(u_ref, bre_ref, bim_ref, pwr_ref, pwi_ref, cre_ref, cim_ref, y_ref, car_ref):
    TB = S5_BLOCK
    f32 = jnp.float32

    @pl.when(pl.program_id(1) == 0)
    def _():
        car_ref[...] = jnp.zeros_like(car_ref)

    u = u_ref[0].astype(jnp.bfloat16)
    xr = jnp.dot(u, bre_ref[0], preferred_element_type=f32)
    xi = jnp.dot(u, bim_ref[0], preferred_element_type=f32)
    rows = lax.broadcasted_iota(jnp.int32, (TB, 1), 0)
    s = 1
    while s < TB:
        ar = pwr_ref[0, s - 1:s, :]
        ai = pwi_ref[0, s - 1:s, :]
        keep = rows >= s
        sr = jnp.where(keep, pltpu.roll(xr, s, axis=0), 0.0)
        si = jnp.where(keep, pltpu.roll(xi, s, axis=0), 0.0)
        xr, xi = xr + (ar * sr - ai * si), xi + (ar * si + ai * sr)
        s *= 2
    cr = car_ref[0:1, :]
    ci = car_ref[1:2, :]
    pr = pwr_ref[0]
    pi_ = pwi_ref[0]
    xr = xr + (pr * cr - pi_ * ci)
    xi = xi + (pr * ci + pi_ * cr)
    car_ref[0:1, :] = xr[TB - 1:TB, :]
    car_ref[1:2, :] = xi[TB - 1:TB, :]
    y_ref[0] = (jnp.dot(xr.astype(jnp.bfloat16), cre_ref[...], preferred_element_type=f32)
                - jnp.dot(xi.astype(jnp.bfloat16), cim_ref[...], preferred_element_type=f32))


def s5_scan_pallas(u, bt_re, bt_im, pw_re, pw_im, c_re, c_im):
    _, T, W = u.shape
    N = bt_re.shape[-1]
    TB = S5_BLOCK
    return pl.pallas_call(
        _s5_scan_kernel,
        grid=(2, T // TB),
        in_specs=[pl.BlockSpec((1, TB, W), lambda d, i: (d, i, 0)),
                  pl.BlockSpec((1, W, N), lambda d, i: (d, 0, 0)),
                  pl.BlockSpec((1, W, N), lambda d, i: (d, 0, 0)),
                  pl.BlockSpec((1, TB, N), lambda d, i: (d, 0, 0)),
                  pl.BlockSpec((1, TB, N), lambda d, i: (d, 0, 0)),
                  pl.BlockSpec((N, W), lambda d, i: (0, 0)),
                  pl.BlockSpec((N, W), lambda d, i: (0, 0))],
        out_specs=pl.BlockSpec((1, TB, W), lambda d, i: (d, i, 0)),
        out_shape=jax.ShapeDtypeStruct((2, T, W), jnp.float32),
        scratch_shapes=[pltpu.VMEM((2, N), jnp.float32)],
        compiler_params=pltpu.CompilerParams(dimension_semantics=("parallel", "arbitrary"),
                                             vmem_limit_bytes=VMEM_LIMIT_BYTES),
    )(u, bt_re, bt_im, pw_re, pw_im, c_re, c_im)


def s5_branch(u_c, u_l, lam_re, lam_im, log_dt, b_re, b_im, c_re, c_im, d_skip, glu_w, glu_b, need_ctx):
    f32 = jnp.float32
    assert u_l.shape[0] == 1
    C = u_c.shape[1]
    G, P, GC = SSM_GROUPS, SSM_STATE, SSM_GROUP
    dt = jnp.exp(log_dt.astype(f32))[..., None]
    lr, li = lam_re.astype(f32), lam_im.astype(f32)
    mag = jnp.exp(dt * lr)
    ab_re, ab_im = mag * jnp.cos(dt * li), mag * jnp.sin(dt * li)
    den = lr * lr + li * li
    gm_re = ((ab_re - 1.0) * lr + ab_im * li) / den
    gm_im = (ab_im * lr - (ab_re - 1.0) * li) / den
    eye_g = jnp.eye(G, dtype=f32)
    br = jnp.transpose(b_re.astype(f32), (0, 2, 1))
    bi = jnp.transpose(b_im.astype(f32), (0, 2, 1))
    gr, gi = gm_re[:, :, None, :], gm_im[:, :, None, :]
    blockdiag = lambda m: jnp.einsum('dgcp,gh->dgchp', m, eye_g).reshape(2, G * GC, G * P)
    bt_re = blockdiag(gr * br - gi * bi).astype(jnp.bfloat16)
    bt_im = blockdiag(gr * bi + gi * br).astype(jnp.bfloat16)
    n = jnp.arange(1, S5_BLOCK + 1, dtype=f32)[None, :, None, None]
    theta = dt * li
    theta = theta - (2.0 * math.pi) * jnp.round(theta / (2.0 * math.pi))
    pmag = jnp.exp(n * (dt * lr)[:, None])
    pw_re = (pmag * jnp.cos(n * theta[:, None])).reshape(2, S5_BLOCK, G * P)
    pw_im = (pmag * jnp.sin(n * theta[:, None])).reshape(2, S5_BLOCK, G * P)
    cbd = lambda m: jnp.einsum('gcp,gh->gphc', m.astype(f32), eye_g).reshape(G * P, G * GC).astype(jnp.bfloat16)
    uc, ul = u_c[0].astype(f32), u_l[0].astype(f32)
    u2 = jnp.stack([jnp.concatenate([uc, ul], axis=0), jnp.concatenate([uc[::-1], ul[::-1]], axis=0)])
    yp = s5_scan_pallas(u2, bt_re, bt_im, pw_re, pw_im, cbd(c_re), cbd(c_im))

    def readout(y, u):
        y = y + d_skip * u.astype(f32)
        z = jax.nn.gelu(y) @ glu_w + glu_b
        za, zb = jnp.split(z, 2, axis=-1)
        return (za * jax.nn.sigmoid(zb)).astype(u.dtype)

    lat = readout((yp[0, C:] + yp[1, C:][::-1])[None], u_l)
    ctx_out = readout((yp[0, :C] + yp[1, :C][::-1])[None], u_c) if need_ctx else None
    return ctx_out, lat


def shift_mix(p, mu_prev, mu_next):
    zero = jnp.zeros_like(p[:, :1])
    prev = jnp.concatenate([zero, p[:, :-1]], axis=1)
    nxt = jnp.concatenate([p[:, 1:], zero], axis=1)
    return p + mu_prev * (prev - p) + mu_next * (nxt - p)


RWKV_CHUNK = 64
RWKV_INV_BLOCK = 16
RWKV_PREP_UNROLL = 4

_NN = (((1,), (0,)), ((), ()))
_NT = (((1,), (1,)), ((), ()))
_TN = (((0,), (0,)), ((), ()))


def _split_bf16(x):
    hi = x.astype(jnp.bfloat16)
    lo = (x - hi.astype(jnp.float32)).astype(jnp.bfloat16)
    return hi, lo


def _dot3(a, b, dims=_NN):
    ah, al = _split_bf16(a)
    bh, bl = _split_bf16(b)
    d = functools.partial(lax.dot_general, dimension_numbers=dims, preferred_element_type=jnp.float32)
    return d(ah, bh) + (d(ah, bl) + d(al, bh))


def _rwkv_chunk_maps(lw, r, kk, be, kd, v):
    L = RWKV_CHUNK
    f32 = jnp.float32
    row = lax.broadcasted_iota(jnp.int32, (L, L), 0)
    col = lax.broadcasted_iota(jnp.int32, (L, L), 1)
    tril_i = row >= col
    tril_s = row > col
    blk = (row // RWKV_INV_BLOCK) == (col // RWKV_INV_BLOCK)
    eye = jnp.where(row == col, 1.0, 0.0).astype(f32)
    tri = jnp.where(tril_i, 1.0, 0.0).astype(jnp.bfloat16)
    h1 = lw.astype(jnp.bfloat16)
    r1 = lw - h1.astype(f32)
    h2 = r1.astype(jnp.bfloat16)
    h3 = (r1 - h2.astype(f32)).astype(jnp.bfloat16)
    dd = functools.partial(jnp.dot, preferred_element_type=f32)
    c = dd(tri, h1) + (dd(tri, h2) + dd(tri, h3))
    c_end = c[L - 1:L, :]
    e_c = jnp.exp(c)
    e_nc = jnp.exp(-c)
    e_end = jnp.exp(c_end)
    al = kk * jnp.exp(c - lw)
    rt = r * e_c
    bb = be * e_nc
    kb = kd * e_nc
    bh = bb * e_end
    kh = kb * e_end
    lhs = jnp.concatenate([al, rt], axis=0)
    g1 = _dot3(lhs, bb, _NT)
    g2 = _dot3(lhs, kb, _NT)
    n = jnp.where(tril_s, g1[:L], 0.0)
    arb = jnp.where(tril_i, g1[L:], 0.0)
    aak = jnp.where(tril_s, g2[:L], 0.0)
    ark = jnp.where(tril_i, g2[L:], 0.0)
    d1 = jnp.where(blk, n, 0.0)
    o1 = n - d1
    d2 = _dot3(d1, d1)
    d4 = _dot3(d2, d2)
    d8 = _dot3(d4, d4)
    imd = eye - d1
    t1 = imd + _dot3(imd, d2)
    t2 = t1 + _dot3(t1, d4)
    dinv = t2 + _dot3(t2, d8)
    e1 = _dot3(dinv, o1)
    e2 = _dot3(e1, e1)
    ime = eye - e1
    fm = ime + _dot3(ime, e2)
    minv = _dot3(fm, dinv)
    at = _dot3(minv, al)
    pm = _dot3(minv, _dot3(aak, v))
    rh = rt - _dot3(arb, at)
    y0 = _dot3(ark, v) - _dot3(arb, pm)
    gt = eye * e_end - _dot3(bh, at, _TN)
    ht = _dot3(kh, v, _TN) - _dot3(bh, pm, _TN)
    return gt, ht, rh, y0


def _rwkv_prep_kernel(lw_ref, r_ref, kk_ref, be_ref, kd_ref, v_ref, gt_ref, ht_ref, rh_ref, y0_ref):
    def heads(i, carry):
        for j in range(RWKV_PREP_UNROLL):
            h = i * RWKV_PREP_UNROLL + j
            gt, ht, rh, y0 = _rwkv_chunk_maps(lw_ref[0, h], r_ref[0, h], kk_ref[0, h], be_ref[0, h],
                                              kd_ref[0, h], v_ref[0, h])
            gt_ref[0, h, 0] = gt
            ht_ref[0, h, 0] = ht
            rh_ref[0, h, 0] = rh
            y0_ref[0, h, 0] = y0
        return carry

    lax.fori_loop(0, RWKV_HEADS // RWKV_PREP_UNROLL, heads, 0)


def _rwkv_scan_kernel(gt_ref, ht_ref, rh_ref, y0_ref, y_ref, st_ref):
    @pl.when(pl.program_id(0) == 0)
    def _():
        st_ref[...] = jnp.zeros_like(st_ref)

    for d in range(2):
        for h in range(RWKV_HEADS):
            st = st_ref[d, h]
            y_ref[d, h] = _dot3(rh_ref[d, h, 0], st) + y0_ref[d, h, 0]
            st_ref[d, h] = _dot3(gt_ref[d, h, 0], st) + ht_ref[d, h, 0]


def rwkv_scan_pallas(lw, r, kk, be, kd, v):
    _, H, T, N = lw.shape
    L = RWKV_CHUNK
    nc = T // L
    in_spec = pl.BlockSpec((1, H, L, N), lambda d, c: (d, 0, c, 0))
    map_spec = pl.BlockSpec((1, H, 1, N, N), lambda d, c: (d, 0, c, 0, 0))
    map_shape = jax.ShapeDtypeStruct((2, H, nc, N, N), jnp.float32)
    gt, ht, rh, y0 = pl.pallas_call(
        _rwkv_prep_kernel,
        grid=(2, nc),
        in_specs=[in_spec] * 6,
        out_specs=[map_spec] * 4,
        out_shape=[map_shape] * 4,
        compiler_params=pltpu.CompilerParams(dimension_semantics=("parallel", "parallel"),
                                             vmem_limit_bytes=VMEM_LIMIT_BYTES),
    )(lw, r, kk, be, kd, v)
    scan_spec = pl.BlockSpec((2, H, 1, N, N), lambda c: (0, 0, c, 0, 0))
    return pl.pallas_call(
        _rwkv_scan_kernel,
        grid=(nc,),
        in_specs=[scan_spec] * 4,
        out_specs=pl.BlockSpec((2, H, L, N), lambda c: (0, 0, c, 0)),
        out_shape=jax.ShapeDtypeStruct((2, H, T, N), jnp.float32),
        scratch_shapes=[pltpu.VMEM((2, H, N, N), jnp.float32)],
        compiler_params=pltpu.CompilerParams(dimension_semantics=("arbitrary",),
                                             vmem_limit_bytes=VMEM_LIMIT_BYTES),
    )(gt, ht, rh, y0)


def rwkv_branch(p_c, p_l, mu_prev, mu_next, w0, w2, a0, a2, g2, k_k, k_a, r_k, lnx_w, lnx_b, need_ctx):
    f32 = jnp.float32
    W = BRANCH_WIDTH
    assert p_l.shape[0] == 1
    C = p_c.shape[1]

    def features(p):
        B, T = p.shape[:2]
        hd = lambda t: t.reshape(B, T, RWKV_HEADS, HEAD_DIM)
        pm = shift_mix(p.astype(f32), mu_prev, mu_next)
        r, k, v, w1f, w1b, a1f, a1b, g1 = split_cols(pm, RWKV_COLS)
        kk = hd(k * k_k)
        kk = kk * lax.rsqrt(jnp.maximum(jnp.sum(kk * kk, axis=-1, keepdims=True), 1e-12))
        dirs = []
        for d, (w1, a1) in enumerate(((w1f, a1f), (w1b, a1b))):
            w = -jax.nn.softplus(-(w0[d] + jnp.tanh(w1) @ w2[d])) - 0.5
            log_decay = -jnp.exp(w)
            a = jax.nn.sigmoid(a0[d] + a1 @ a2[d])
            k_d = k * (1.0 + (a - 1.0) * k_a)
            dirs.append((hd(log_decay), hd(k_d), hd(a)))
        g = jax.nn.sigmoid(g1) @ g2
        return hd(r), hd(v), kk, dirs, g

    def readout(y, r, v, dirs, g, dtype):
        B, T = y.shape[:2]
        mu = jnp.mean(y, axis=-1, keepdims=True)
        var = jnp.mean(jnp.square(y - mu), axis=-1, keepdims=True)
        yn = ((y - mu) * lax.rsqrt(var + RWKV_GN_EPS)).reshape(B, T, W) * lnx_w + lnx_b
        bonus = jnp.sum(r * (dirs[0][1] + dirs[1][1]) * r_k, axis=-1, keepdims=True) * v
        return ((yn + bonus.reshape(B, T, W)) * g).astype(dtype)

    rc, vc, kkc, dirs_c, gc = features(p_c)
    rl, vl, kkl, dirs_l, gl = features(p_l)

    def order(tc, tl):
        fwd = jnp.concatenate([tc[0], tl[0]], axis=0)
        bwd = jnp.concatenate([tc[0, ::-1], tl[0, ::-1]], axis=0)
        return jnp.stack([fwd, bwd]).transpose(0, 2, 1, 3)

    def order2(tc0, tl0, tc1, tl1):
        fwd = jnp.concatenate([tc0[0], tl0[0]], axis=0)
        bwd = jnp.concatenate([tc1[0, ::-1], tl1[0, ::-1]], axis=0)
        return jnp.stack([fwd, bwd]).transpose(0, 2, 1, 3)

    lw = order2(dirs_c[0][0], dirs_l[0][0], dirs_c[1][0], dirs_l[1][0])
    kd = order2(dirs_c[0][1], dirs_l[0][1], dirs_c[1][1], dirs_l[1][1])
    be = order2(kkc * dirs_c[0][2], kkl * dirs_l[0][2], kkc * dirs_c[1][2], kkl * dirs_l[1][2])
    y = rwkv_scan_pallas(lw, order(rc, rl), order(kkc, kkl), be, kd, order(vc, vl))
    y = y.transpose(0, 2, 1, 3)
    y_c = (y[0, :C] + y[1, :C][::-1])[None]
    y_l = (y[0, C:] + y[1, C:][::-1])[None]
    lat = readout(y_l, rl, vl, dirs_l, gl, p_l.dtype)
    ctx_out = readout(y_c, rc, vc, dirs_c, gc, p_c.dtype) if need_ctx else None
    return ctx_out, lat


HGRN_SUB = 16
HGRN_BLOCK = 128


def _hgrn_kernel(q_ref, lf_ref, k_ref, v_ref, seg_ref, o_ref, st_ref):
    TB, L = HGRN_BLOCK, HGRN_SUB
    f32 = jnp.float32
    bf16 = jnp.bfloat16

    @pl.when(pl.program_id(1) == 0)
    def _():
        st_ref[...] = jnp.zeros_like(st_ref)

    q = q_ref[0]
    k = k_ref[0]
    v = v_ref[0]
    seg = seg_ref[...]
    pos = lax.broadcasted_iota(jnp.int32, (TB, 1), 0) % L
    b = lf_ref[0]
    s = 1
    while s < L:
        b = b + jnp.where(pos >= s, pltpu.roll(b, s, axis=0), 0.0)
        s *= 2
    acc = jnp.zeros_like(q)
    for j in range(L):
        if j == 0:
            z = q * k
            vj = v
        else:
            keep = pos >= j
            ex = jnp.exp(jnp.where(keep, b - pltpu.roll(b, j, axis=0), 0.0))
            z = jnp.where(keep, q * pltpu.roll(k, j, axis=0) * ex, 0.0)
            vj = pltpu.roll(v, j, axis=0)
        zh = z.astype(bf16)
        zl = (z - zh.astype(f32)).astype(bf16)
        a = jnp.dot(zh, seg, preferred_element_type=f32) + jnp.dot(zl, seg, preferred_element_type=f32)
        acc = acc + a * vj
    o_ref[0] = acc
    eb = jnp.exp(b)
    qs = (q * eb).astype(bf16)
    same_head = seg > 0
    for i in range(TB // L):
        rows = slice(i * L, (i + 1) * L)
        b_end = b[(i + 1) * L - 1:(i + 1) * L, :]
        st = st_ref[...]
        o_ref[0, rows, :] += lax.dot_general(qs[rows], st.astype(bf16), _NT, preferred_element_type=f32)
        kt = (k[rows] * jnp.exp(b_end - b[rows])).astype(bf16)
        upd = lax.dot_general(v[rows].astype(bf16), kt, _TN, preferred_element_type=f32)
        st_ref[...] = st * jnp.exp(b_end) + jnp.where(same_head, upd, 0.0)


def hgrn_scan_pallas(q, lf, k, v):
    _, T, W = q.shape
    TB = HGRN_BLOCK
    head = jnp.arange(W, dtype=jnp.int32) // HEAD_DIM
    seg = (head[:, None] == head[None, :]).astype(jnp.bfloat16)
    spec = pl.BlockSpec((1, TB, W), lambda d, i: (d, i, 0))
    return pl.pallas_call(
        _hgrn_kernel,
        grid=(2, T // TB),
        in_specs=[spec, spec, spec, spec, pl.BlockSpec((W, W), lambda d, i: (0, 0))],
        out_specs=spec,
        out_shape=jax.ShapeDtypeStruct((2, T, W), jnp.float32),
        scratch_shapes=[pltpu.VMEM((W, W), jnp.float32)],
        compiler_params=pltpu.CompilerParams(dimension_semantics=("parallel", "arbitrary"),
                                             vmem_limit_bytes=VMEM_LIMIT_BYTES),
    )(q, lf, k, v, seg)


def hgrn_branch(p_c, p_l, lb, norm_w, need_ctx):
    f32 = jnp.float32
    log_lb, log_ub = jnp.log(lb), jnp.log1p(-lb)

    def features(p):
        B, T = p.shape[:2]
        hd = lambda t: t.reshape(B, T, HGRN_HEADS, HEAD_DIM)
        q, zf, zb, i, g = split_cols(p.astype(f32), HGRN_COLS)
        gates = []
        for z in (zf, zb):
            log_f = jnp.logaddexp(log_lb, log_ub + jax.nn.log_sigmoid(z))
            k = (1.0 - lb) * jax.nn.sigmoid(-z)
            gates.append((hd(log_f), hd(k)))
        return hd(q), hd(i), gates, g

    def readout(o, g, dtype):
        B, T = o.shape[:2]
        return (rms_norm(o, norm_w).reshape(B, T, BRANCH_WIDTH) * jax.nn.silu(g)).astype(dtype)

    qc, ic, gates_c, gc = features(p_c)
    ql, il, gates_l, gl = features(p_l)
    assert p_l.shape[0] == 1
    C = p_c.shape[1]
    W = BRANCH_WIDTH

    def order(tc0, tl0, tc1, tl1):
        fwd = jnp.concatenate([tc0[0], tl0[0]], axis=0)
        bwd = jnp.concatenate([tc1[0, ::-1], tl1[0, ::-1]], axis=0)
        return jnp.stack([fwd, bwd]).reshape(2, -1, W)

    o = hgrn_scan_pallas(order(qc, ql, qc, ql),
                         order(gates_c[0][0], gates_l[0][0], gates_c[1][0], gates_l[1][0]),
                         order(gates_c[0][1], gates_l[0][1], gates_c[1][1], gates_l[1][1]),
                         order(ic, il, ic, il))
    o = o.reshape(2, -1, HGRN_HEADS, HEAD_DIM)
    lat = readout((o[0, C:] + o[1, C:][::-1])[None], gl, p_l.dtype)
    ctx_out = readout((o[0, :C] + o[1, :C][::-1])[None], gc, p_c.dtype) if need_ctx else None
    return ctx_out, lat


def gated_merge(gate_cols, branches, b_merge, w_branch, w_out):
    B, T = gate_cols.shape[:2]
    gates = jax.nn.sigmoid(gate_cols + b_merge).reshape(B, T, N_BRANCH, D_MODEL)
    br = jnp.stack(branches, axis=2)
    proj = jnp.einsum('btnc,ncd->btnd', br, w_branch)
    return jnp.sum(gates * proj, axis=2) @ w_out


MOE_ROWS = 256


def _moe_expert_kernel(be_ref, nb_ref, x_ref, wg_ref, wu_ref, wd_ref, bg_ref, bu_ref, bd_ref, y_ref):
    f32 = jnp.float32

    @pl.when(pl.program_id(0) < nb_ref[0])
    def _():
        x = x_ref[...]
        gate = jnp.dot(x, wg_ref[0], preferred_element_type=f32) + bg_ref[0]
        up = jnp.dot(x, wu_ref[0], preferred_element_type=f32) + bu_ref[0]
        gate = jnp.minimum(gate, SWIGLU_LIMIT)
        up = jnp.clip(up, -SWIGLU_LIMIT, SWIGLU_LIMIT)
        glu = gate * jax.nn.sigmoid(SWIGLU_ALPHA * gate)
        act = ((up + 1.0) * glu).astype(jnp.bfloat16)
        y_ref[...] = jnp.dot(act, wd_ref[0], preferred_element_type=f32) + bd_ref[0]

    @pl.when(pl.program_id(0) >= nb_ref[0])
    def _():
        y_ref[...] = jnp.zeros_like(y_ref)


def moe_experts_pallas(xb, block_e, n_used, wg, wu, wd, bg, bu, bd):
    cap, D = xb.shape
    F = wg.shape[-1]
    R = MOE_ROWS
    n_blocks = cap // R
    grid_spec = pltpu.PrefetchScalarGridSpec(
        num_scalar_prefetch=2,
        grid=(n_blocks,),
        in_specs=[pl.BlockSpec((R, D), lambda i, be, nb: (i, 0)),
                  pl.BlockSpec((1, D, F), lambda i, be, nb: (be[i], 0, 0)),
                  pl.BlockSpec((1, D, F), lambda i, be, nb: (be[i], 0, 0)),
                  pl.BlockSpec((1, F, D), lambda i, be, nb: (be[i], 0, 0)),
                  pl.BlockSpec((1, 1, F), lambda i, be, nb: (be[i], 0, 0)),
                  pl.BlockSpec((1, 1, F), lambda i, be, nb: (be[i], 0, 0)),
                  pl.BlockSpec((1, 1, D), lambda i, be, nb: (be[i], 0, 0))],
        out_specs=pl.BlockSpec((R, D), lambda i, be, nb: (i, 0)),
    )
    return pl.pallas_call(
        _moe_expert_kernel,
        grid_spec=grid_spec,
        out_shape=jax.ShapeDtypeStruct((cap, D), jnp.float32),
        compiler_params=pltpu.CompilerParams(dimension_semantics=("arbitrary",),
                                             vmem_limit_bytes=VMEM_LIMIT_BYTES),
    )(block_e, n_used, xb, wg, wu, wd, bg, bu, bd)


def moe_ffn(h, router_w, router_b, gu_w, gu_b, dn_w, dn_b):
    N, D = h.shape
    R = MOE_ROWS
    logits = (jnp.dot(h, router_w, precision=lax.Precision.HIGHEST) + router_b).astype(jnp.float32)
    top_l, top_e = lax.top_k(logits, TOP_K)
    weights = jax.nn.softmax(top_l, axis=-1)
    n_assign = N * TOP_K
    flat_e = top_e.reshape(-1)
    order = jnp.argsort(flat_e)
    e_sorted = flat_e[order]
    tok_sorted = (jnp.arange(n_assign, dtype=jnp.int32) // TOP_K)[order]
    counts = jnp.bincount(flat_e, length=N_EXPERTS)
    padded = (counts + R - 1) // R * R
    start_sorted = jnp.cumsum(counts) - counts
    pad_end = jnp.cumsum(padded)
    start_pad = pad_end - padded
    dest = (start_pad[e_sorted] + jnp.arange(n_assign, dtype=jnp.int32) - start_sorted[e_sorted]).astype(jnp.int32)
    n_blocks = -(-(n_assign + N_EXPERTS * (R - 1)) // R)
    cap = n_blocks * R
    slot_tok = jnp.full((cap,), N, dtype=jnp.int32).at[dest].set(tok_sorted)
    slot_of_assign = jnp.zeros((n_assign,), jnp.int32).at[order].set(dest)
    block_e = jnp.minimum(jnp.searchsorted(pad_end, jnp.arange(n_blocks, dtype=jnp.int32) * R, side='right'),
                          N_EXPERTS - 1).astype(jnp.int32)
    n_used = (pad_end[-1] // R).astype(jnp.int32).reshape(1)
    h_pad = jnp.concatenate([h.astype(jnp.bfloat16), jnp.zeros((1, D), jnp.bfloat16)], axis=0)
    xb = h_pad[slot_tok]
    bf16 = jnp.bfloat16
    yb = moe_experts_pallas(xb, block_e, n_used,
                            gu_w[:, :, ::2].astype(bf16), gu_w[:, :, 1::2].astype(bf16), dn_w.astype(bf16),
                            gu_b[:, None, ::2], gu_b[:, None, 1::2], dn_b[:, None, :])
    picked = yb[slot_of_assign].reshape(N, TOP_K, D)
    return jnp.sum(picked * weights.astype(h.dtype)[:, :, None], axis=1)


def _mm_kernel(a_ref, b_ref, o_ref):
    o_ref[...] = jnp.dot(a_ref[...].astype(jnp.bfloat16), b_ref[...].astype(jnp.bfloat16),
                         preferred_element_type=jnp.float32)


def _pick(n, cands):
    for c in cands:
        if n % c == 0:
            return c
    return n


def pallas_matmul(a, b):
    M, K = a.shape
    N = b.shape[1]
    tm = _pick(M, (768, 512, 384, 256, 128))
    tn = _pick(N, (512, 384, 256, 128))
    return pl.pallas_call(
        _mm_kernel,
        grid=(M // tm, N // tn),
        in_specs=[pl.BlockSpec((tm, K), lambda i, j: (i, 0)),
                  pl.BlockSpec((K, tn), lambda i, j: (0, j))],
        out_specs=pl.BlockSpec((tm, tn), lambda i, j: (i, j)),
        out_shape=jax.ShapeDtypeStruct((M, N), jnp.float32),
        compiler_params=pltpu.CompilerParams(dimension_semantics=("parallel", "parallel"),
                                             vmem_limit_bytes=VMEM_LIMIT_BYTES),
    )(a, b)


def kernel(x, c, ctx, c_ctx, ada_w, ada_b, norm_mix, norm_ffn, w_in, b_merge, qk_norm_q, qk_norm_k, att_sink,
           ssm_lam_re, ssm_lam_im, ssm_log_dt, ssm_b_re, ssm_b_im, ssm_c_re, ssm_c_im, ssm_d, ssm_glu_w, ssm_glu_b,
           rwkv_mu_prev, rwkv_mu_next, rwkv_w0, rwkv_w2, rwkv_a0, rwkv_a2, rwkv_g2, rwkv_k_k, rwkv_k_a, rwkv_r_k,
           rwkv_lnx_w, rwkv_lnx_b, hgrn_lb_raw, hgrn_norm, w_branch, w_out, router_w, router_b,
           exp_gu_w, exp_gu_b, exp_down_w, exp_down_b):
    B, S, D = x.shape
    C = ctx.shape[1]
    rows = S // GRID_W
    cos, sin = axial_rope(rows)
    lb_all = jnp.cumsum(jax.nn.softmax(hgrn_lb_raw.astype(jnp.float32), axis=0), axis=0)
    lb_all = lb_all - lb_all[:1]
    h_ctx = ctx
    for l in range(DEPTH):
        need_ctx = l < DEPTH - 1
        m_lat = [t[:, None, :] for t in jnp.split(jax.nn.silu(c) @ ada_w[l] + ada_b[l], 6, axis=-1)]
        m_ctx = jnp.split(jax.nn.silu(c_ctx) @ ada_w[l] + ada_b[l], 6, axis=-1)
        hl = modulate(rms_norm(x, norm_mix[l]), m_lat[0], m_lat[1])
        hc = modulate(rms_norm(h_ctx, norm_mix[l]), m_ctx[0], m_ctx[1])
        ql_, kl_, vl_, ul, rwl, hgl, gtl = split_cols(pallas_matmul(hl[0], w_in[l])[None], IN_COLS)
        qc_, kc_, vc_, uc, rwc, hgc, gtc = split_cols(pallas_matmul(hc[0], w_in[l])[None], IN_COLS)
        att_c, att_l = attention_branch(qc_, kc_, vc_, ql_, kl_, vl_, qk_norm_q[l], qk_norm_k[l], att_sink[l],
                                        cos, sin, need_ctx)
        ssm_c, ssm_l = s5_branch(uc, ul, ssm_lam_re[l], ssm_lam_im[l], ssm_log_dt[l], ssm_b_re[l], ssm_b_im[l],
                                 ssm_c_re[l], ssm_c_im[l], ssm_d[l], ssm_glu_w[l], ssm_glu_b[l], need_ctx)
        rw_c, rw_l = rwkv_branch(rwc, rwl, rwkv_mu_prev[l], rwkv_mu_next[l], rwkv_w0[l], rwkv_w2[l], rwkv_a0[l],
                                 rwkv_a2[l], rwkv_g2[l], rwkv_k_k[l], rwkv_k_a[l], rwkv_r_k[l], rwkv_lnx_w[l],
                                 rwkv_lnx_b[l], need_ctx)
        hg_c, hg_l = hgrn_branch(hgc, hgl, lb_all[l], hgrn_norm[l], need_ctx)
        x = x + m_lat[2] * gated_merge(gtl, (att_l, ssm_l, rw_l, hg_l), b_merge[l], w_branch[l], w_out[l])
        hl2 = modulate(rms_norm(x, norm_ffn[l]), m_lat[3], m_lat[4])
        moe_args = (router_w[l], router_b[l], exp_gu_w[l], exp_gu_b[l], exp_down_w[l], exp_down_b[l])
        if need_ctx:
            h_ctx = h_ctx + m_ctx[2] * gated_merge(gtc, (att_c, ssm_c, rw_c, hg_c), b_merge[l], w_branch[l], w_out[l])
            hc2 = modulate(rms_norm(h_ctx, norm_ffn[l]), m_ctx[3], m_ctx[4])
            tokens = jnp.concatenate([hc2.reshape(B * C, D), hl2.reshape(B * S, D)], axis=0)
            y = moe_ffn(tokens, *moe_args)
            h_ctx = h_ctx + m_ctx[5] * y[:B * C].reshape(B, C, D)
            x = x + m_lat[5] * y[B * C:].reshape(B, S, D)
        else:
            y = moe_ffn(hl2.reshape(B * S, D), *moe_args)
            x = x + m_lat[5] * y.reshape(B, S, D)
    return x
```

```python
import functools
import math
import jax
import jax.numpy as jnp
from jax import lax
import numpy as np
from jax.experimental import pallas as pl
from jax.experimental.pallas import tpu as pltpu

D_MODEL = 2048
BATCH = 1
SEQ = 8192
DEPTH = 2

GRID_W = 64
CTX_LEN = 256
HEAD_DIM = 64
N_BRANCH = 4
BRANCH_WIDTH = D_MODEL // 4

ATT_HEADS = BRANCH_WIDTH // HEAD_DIM
ATT_KV_HEADS = ATT_HEADS // 4
ATT_GROUP = ATT_HEADS // ATT_KV_HEADS
WINDOW = 128
ATT_BLOCK = 128
ROPE_BASE = 10000.0

SSM_GROUP = 16
SSM_GROUPS = BRANCH_WIDTH // SSM_GROUP
SSM_STATE = 64

RWKV_HEADS = BRANCH_WIDTH // HEAD_DIM
DECAY_LORA = 64
ICLR_LORA = 64
GATE_LORA = 128
RWKV_GN_EPS = 64e-5

HGRN_HEADS = BRANCH_WIDTH // HEAD_DIM
HGRN_CHUNK = 64

N_EXPERTS = 32
TOP_K = 4
D_EXPERT = D_MODEL // 2
SWIGLU_LIMIT = 7.0
SWIGLU_ALPHA = 1.702
MOE_BLOCK = 128

NORM_EPS = 1e-6

ATT_COLS = (ATT_HEADS * HEAD_DIM, ATT_KV_HEADS * HEAD_DIM, ATT_KV_HEADS * HEAD_DIM)
RWKV_COLS = (BRANCH_WIDTH, BRANCH_WIDTH, BRANCH_WIDTH, DECAY_LORA, DECAY_LORA, ICLR_LORA, ICLR_LORA, GATE_LORA)
HGRN_COLS = (BRANCH_WIDTH,) * 5
IN_COLS = ATT_COLS + (BRANCH_WIDTH, sum(RWKV_COLS), sum(HGRN_COLS), N_BRANCH * D_MODEL)
D_IN = sum(IN_COLS)

VMEM_LIMIT_BYTES = 48 * 1024 * 1024
MOE_VMEM_LIMIT_BYTES = 56 * 1024 * 1024


def split_cols(p, sizes):
    idx = [sum(sizes[:i + 1]) for i in range(len(sizes) - 1)]
    return jnp.split(p, idx, axis=-1)


def rms_norm(x, w):
    xf = x.astype(jnp.float32)
    y = xf * lax.rsqrt(jnp.mean(xf * xf, axis=-1, keepdims=True) + NORM_EPS)
    return (y * w).astype(x.dtype)


def modulate(h, shift, scale):
    return h * (1.0 + scale) + shift


def axial_rope(rows):
    n_freq = HEAD_DIM // 4
    inv = ROPE_BASE ** (-jnp.arange(n_freq, dtype=jnp.float32) / n_freq)
    row = jnp.repeat(jnp.arange(rows, dtype=jnp.float32), GRID_W)
    col = jnp.tile(jnp.arange(GRID_W, dtype=jnp.float32), rows)
    ang = jnp.concatenate([row[:, None] * inv, col[:, None] * inv], axis=-1)
    return jnp.cos(ang), jnp.sin(ang)


def apply_rope(x, cos, sin):
    xf = x.astype(jnp.float32)
    half = HEAD_DIM // 2
    x1, x2 = xf[..., :half], xf[..., half:]
    cs, sn = cos[None, :, None, :], sin[None, :, None, :]
    return jnp.concatenate([x1 * cs - x2 * sn, x1 * sn + x2 * cs], axis=-1).astype(x.dtype)


def softmax_with_sink(logits, sink):
    m = jnp.maximum(jnp.max(logits, axis=-1, keepdims=True), sink)
    e = jnp.exp(logits - m)
    return e / (jnp.sum(e, axis=-1, keepdims=True) + jnp.exp(sink - m))


def attention_branch(q_c, k_c, v_c, q_l, k_l, v_l, qn_w, kn_w, sink, cos, sin, need_ctx):
    B, S = q_l.shape[:2]
    C = q_c.shape[1]
    heads = lambda t, h: t.reshape(t.shape[0], t.shape[1], h, HEAD_DIM)
    kc = rms_norm(heads(k_c, ATT_KV_HEADS), kn_w)
    vc = heads(v_c, ATT_KV_HEADS)
    ql = apply_rope(rms_norm(heads(q_l, ATT_HEADS), qn_w), cos, sin)
    kl = apply_rope(rms_norm(heads(k_l, ATT_KV_HEADS), kn_w), cos, sin)
    vl = heads(v_l, ATT_KV_HEADS)
    scale = HEAD_DIM ** -0.5
    sink_f = sink.astype(jnp.float32).reshape(ATT_KV_HEADS, ATT_GROUP)
    nb = S // ATT_BLOCK
    qb = ql.reshape(B, nb, ATT_BLOCK, ATT_KV_HEADS, ATT_GROUP, HEAD_DIM)

    def band(t):
        tp = jnp.pad(t, ((0, 0), (ATT_BLOCK, ATT_BLOCK), (0, 0), (0, 0)))
        tp = tp.reshape(B, nb + 2, ATT_BLOCK, ATT_KV_HEADS, HEAD_DIM)
        return jnp.concatenate([tp[:, :-2], tp[:, 1:-1], tp[:, 2:]], axis=2)

    kb, vb = band(kl), band(vl)
    s_win = jnp.einsum('bnqhgd,bnkhd->bnhgqk', qb, kb, preferred_element_type=jnp.float32) * scale
    q_pos = jnp.arange(nb)[:, None] * ATT_BLOCK + jnp.arange(ATT_BLOCK)[None, :]
    k_pos = jnp.arange(nb)[:, None] * ATT_BLOCK + jnp.arange(3 * ATT_BLOCK)[None, :] - ATT_BLOCK
    kp = k_pos[:, None, :]
    valid = (jnp.abs(kp - q_pos[:, :, None]) <= WINDOW) & (kp >= 0) & (kp < S)
    s_win = jnp.where(valid[None, :, None, None], s_win, -jnp.inf)
    s_ctx = jnp.einsum('bnqhgd,bkhd->bnhgqk', qb, kc, preferred_element_type=jnp.float32) * scale
    p = softmax_with_sink(jnp.concatenate([s_win, s_ctx], axis=-1), sink_f[None, None, :, :, None, None])
    p = p.astype(vl.dtype)
    o = (jnp.einsum('bnhgqk,bnkhd->bnqhgd', p[..., :3 * ATT_BLOCK], vb)
         + jnp.einsum('bnhgqk,bkhd->bnqhgd', p[..., 3 * ATT_BLOCK:], vc))
    lat = o.reshape(B, S, ATT_HEADS * HEAD_DIM)
    ctx_out = None
    if need_ctx:
        qc = rms_norm(heads(q_c, ATT_HEADS), qn_w).reshape(B, C, ATT_KV_HEADS, ATT_GROUP, HEAD_DIM)
        s_cc = jnp.einsum('bqhgd,bkhd->bhgqk', qc, kc, preferred_element_type=jnp.float32) * scale
        pc = softmax_with_sink(s_cc, sink_f[None, :, :, None, None]).astype(vc.dtype)
        ctx_out = jnp.einsum('bhgqk,bkhd->bqhgd', pc, vc).reshape(B, C, ATT_HEADS * HEAD_DIM)
    return ctx_out, lat


S5_BLOCK = 128


def _s5_scan_kernel(u_ref, bre_ref, bim_ref, pwr_ref, pwi_ref, cre_ref, cim_ref, y_ref, car_ref):
    TB = S5_BLOCK
    f32 = jnp.float32

    @pl.when(pl.program_id(1) == 0)
    def _():
        car_ref[...] = jnp.zeros_like(car_ref)

    u = u_ref[0].astype(jnp.bfloat16)
    xr = jnp.dot(u, bre_ref[0], preferred_element_type=f32)
    xi = jnp.dot(u, bim_ref[0], preferred_element_type=f32)
    rows = lax.broadcasted_iota(jnp.int32, (TB, 1), 0)
    s = 1
    while s < TB:
        ar = pwr_ref[0, s - 1:s, :]
        ai = pwi_ref[0, s - 1:s, :]
        keep = rows >= s
        sr = jnp.where(keep, pltpu.roll(xr, s, axis=0), 0.0)
        si = jnp.where(keep, pltpu.roll(xi, s, axis=0), 0.0)
        xr, xi = xr + (ar * sr - ai * si), xi + (ar * si + ai * sr)
        s *= 2
    cr = car_ref[0:1, :]
    ci = car_ref[1:2, :]
    pr = pwr_ref[0]
    pi_ = pwi_ref[0]
    xr = xr + (pr * cr - pi_ * ci)
    xi = xi + (pr * ci + pi_ * cr)
    car_ref[0:1, :] = xr[TB - 1:TB, :]
    car_ref[1:2, :] = xi[TB - 1:TB, :]
    y_ref[0] = (jnp.dot(xr.astype(jnp.bfloat16), cre_ref[...], preferred_element_type=f32)
                - jnp.dot(xi.astype(jnp.bfloat16), cim_ref[...], preferred_element_type=f32))


def s5_scan_pallas(u, bt_re, bt_im, pw_re, pw_im, c_re, c_im):
    _, T, W = u.shape
    N = bt_re.shape[-1]
    TB = S5_BLOCK
    return pl.pallas_call(
        _s5_scan_kernel,
        grid=(2, T // TB),
        in_specs=[pl.BlockSpec((1, TB, W), lambda d, i: (d, i, 0)),
                  pl.BlockSpec((1, W, N), lambda d, i: (d, 0, 0)),
                  pl.BlockSpec((1, W, N), lambda d, i: (d, 0, 0)),
                  pl.BlockSpec((1, TB, N), lambda d, i: (d, 0, 0)),
                  pl.BlockSpec((1, TB, N), lambda d, i: (d, 0, 0)),
                  pl.BlockSpec((N, W), lambda d, i: (0, 0)),
                  pl.BlockSpec((N, W), lambda d, i: (0, 0))],
        out_specs=pl.BlockSpec((1, TB, W), lambda d, i: (d, i, 0)),
        out_shape=jax.ShapeDtypeStruct((2, T, W), jnp.float32),
        scratch_shapes=[pltpu.VMEM((2, N), jnp.float32)],
        compiler_params=pltpu.CompilerParams(dimension_semantics=("parallel", "arbitrary"),
                                             vmem_limit_bytes=VMEM_LIMIT_BYTES),
    )(u, bt_re, bt_im, pw_re, pw_im, c_re, c_im)


def s5_branch(u_c, u_l, lam_re, lam_im, log_dt, b_re, b_im, c_re, c_im, d_skip, glu_w, glu_b, need_ctx):
    f32 = jnp.float32
    assert u_l.shape[0] == 1
    C = u_c.shape[1]
    G, P, GC = SSM_GROUPS, SSM_STATE, SSM_GROUP
    dt = jnp.exp(log_dt.astype(f32))[..., None]
    lr, li = lam_re.astype(f32), lam_im.astype(f32)
    mag = jnp.exp(dt * lr)
    ab_re, ab_im = mag * jnp.cos(dt * li), mag * jnp.sin(dt * li)
    den = lr * lr + li * li
    gm_re = ((ab_re - 1.0) * lr + ab_im * li) / den
    gm_im = (ab_im * lr - (ab_re - 1.0) * li) / den
    eye_g = jnp.eye(G, dtype=f32)
    br = jnp.transpose(b_re.astype(f32), (0, 2, 1))
    bi = jnp.transpose(b_im.astype(f32), (0, 2, 1))
    gr, gi = gm_re[:, :, None, :], gm_im[:, :, None, :]
    blockdiag = lambda m: jnp.einsum('dgcp,gh->dgchp', m, eye_g).reshape(2, G * GC, G * P)
    bt_re = blockdiag(gr * br - gi * bi).astype(jnp.bfloat16)
    bt_im = blockdiag(gr * bi + gi * br).astype(jnp.bfloat16)
    n = jnp.arange(1, S5_BLOCK + 1, dtype=f32)[None, :, None, None]
    theta = dt * li
    theta = theta - (2.0 * math.pi) * jnp.round(theta / (2.0 * math.pi))
    pmag = jnp.exp(n * (dt * lr)[:, None])
    pw_re = (pmag * jnp.cos(n * theta[:, None])).reshape(2, S5_BLOCK, G * P)
    pw_im = (pmag * jnp.sin(n * theta[:, None])).reshape(2, S5_BLOCK, G * P)
    cbd = lambda m: jnp.einsum('gcp,gh->gphc', m.astype(f32), eye_g).reshape(G * P, G * GC).astype(jnp.bfloat16)
    uc, ul = u_c[0].astype(f32), u_l[0].astype(f32)
    u2 = jnp.stack([jnp.concatenate([uc, ul], axis=0), jnp.concatenate([uc[::-1], ul[::-1]], axis=0)])
    yp = s5_scan_pallas(u2, bt_re, bt_im, pw_re, pw_im, cbd(c_re), cbd(c_im))

    def readout(y, u):
        y = y + d_skip * u.astype(f32)
        z = jax.nn.gelu(y) @ glu_w + glu_b
        za, zb = jnp.split(z, 2, axis=-1)
        return (za * jax.nn.sigmoid(zb)).astype(u.dtype)

    lat = readout((yp[0, C:] + yp[1, C:][::-1])[None], u_l)
    ctx_out = readout((yp[0, :C] + yp[1, :C][::-1])[None], u_c) if need_ctx else None
    return ctx_out, lat


def shift_mix(p, mu_prev, mu_next):
    zero = jnp.zeros_like(p[:, :1])
    prev = jnp.concatenate([zero, p[:, :-1]], axis=1)
    nxt = jnp.concatenate([p[:, 1:], zero], axis=1)
    return p + mu_prev * (prev - p) + mu_next * (nxt - p)


RWKV_CHUNK = 64
RWKV_INV_BLOCK = 16
RWKV_PREP_UNROLL = 4

_NN = (((1,), (0,)), ((), ()))
_NT = (((1,), (1,)), ((), ()))
_TN = (((0,), (0,)), ((), ()))


def _split_bf16(x):
    hi = x.astype(jnp.bfloat16)
    lo = (x - hi.astype(jnp.float32)).astype(jnp.bfloat16)
    return hi, lo


def _dot3(a, b, dims=_NN):
    ah, al = _split_bf16(a)
    bh, bl = _split_bf16(b)
    d = functools.partial(lax.dot_general, dimension_numbers=dims, preferred_element_type=jnp.float32)
    return d(ah, bh) + (d(ah, bl) + d(al, bh))


def _rwkv_chunk_maps(lw, r, kk, be, kd, v):
    L = RWKV_CHUNK
    f32 = jnp.float32
    row = lax.broadcasted_iota(jnp.int32, (L, L), 0)
    col = lax.broadcasted_iota(jnp.int32, (L, L), 1)
    tril_i = row >= col
    tril_s = row > col
    blk = (row // RWKV_INV_BLOCK) == (col // RWKV_INV_BLOCK)
    eye = jnp.where(row == col, 1.0, 0.0).astype(f32)
    tri = jnp.where(tril_i, 1.0, 0.0).astype(jnp.bfloat16)
    h1 = lw.astype(jnp.bfloat16)
    r1 = lw - h1.astype(f32)
    h2 = r1.astype(jnp.bfloat16)
    h3 = (r1 - h2.astype(f32)).astype(jnp.bfloat16)
    dd = functools.partial(jnp.dot, preferred_element_type=f32)
    c = dd(tri, h1) + (dd(tri, h2) + dd(tri, h3))
    c_end = c[L - 1:L, :]
    e_c = jnp.exp(c)
    e_nc = jnp.exp(-c)
    e_end = jnp.exp(c_end)
    al = kk * jnp.exp(c - lw)
    rt = r * e_c
    bb = be * e_nc
    kb = kd * e_nc
    bh = bb * e_end
    kh = kb * e_end
    lhs = jnp.concatenate([al, rt], axis=0)
    g1 = _dot3(lhs, bb, _NT)
    g2 = _dot3(lhs, kb, _NT)
    n = jnp.where(tril_s, g1[:L], 0.0)
    arb = jnp.where(tril_i, g1[L:], 0.0)
    aak = jnp.where(tril_s, g2[:L], 0.0)
    ark = jnp.where(tril_i, g2[L:], 0.0)
    d1 = jnp.where(blk, n, 0.0)
    o1 = n - d1
    d2 = _dot3(d1, d1)
    d4 = _dot3(d2, d2)
    d8 = _dot3(d4, d4)
    imd = eye - d1
    t1 = imd + _dot3(imd, d2)
    t2 = t1 + _dot3(t1, d4)
    dinv = t2 + _dot3(t2, d8)
    e1 = _dot3(dinv, o1)
    e2 = _dot3(e1, e1)
    ime = eye - e1
    fm = ime + _dot3(ime, e2)
    minv = _dot3(fm, dinv)
    at = _dot3(minv, al)
    pm = _dot3(minv, _dot3(aak, v))
    rh = rt - _dot3(arb, at)
    y0 = _dot3(ark, v) - _dot3(arb, pm)
    gt = eye * e_end - _dot3(bh, at, _TN)
    ht = _dot3(kh, v, _TN) - _dot3(bh, pm, _TN)
    return gt, ht, rh, y0


def _rwkv_prep_kernel(lw_ref, r_ref, kk_ref, be_ref, kd_ref, v_ref, gt_ref, ht_ref, rh_ref, y0_ref):
    def heads(i, carry):
        for j in range(RWKV_PREP_UNROLL):
            h = i * RWKV_PREP_UNROLL + j
            gt, ht, rh, y0 = _rwkv_chunk_maps(lw_ref[0, h], r_ref[0, h], kk_ref[0, h], be_ref[0, h],
                                              kd_ref[0, h], v_ref[0, h])
            gt_ref[0, h, 0] = gt
            ht_ref[0, h, 0] = ht
            rh_ref[0, h, 0] = rh
            y0_ref[0, h, 0] = y0
        return carry

    lax.fori_loop(0, RWKV_HEADS // RWKV_PREP_UNROLL, heads, 0)


def _rwkv_scan_kernel(gt_ref, ht_ref, rh_ref, y0_ref, y_ref, st_ref):
    @pl.when(pl.program_id(0) == 0)
    def _():
        st_ref[...] = jnp.zeros_like(st_ref)

    for d in range(2):
        for h in range(RWKV_HEADS):
            st = st_ref[d, h]
            y_ref[d, h] = _dot3(rh_ref[d, h, 0], st) + y0_ref[d, h, 0]
            st_ref[d, h] = _dot3(gt_ref[d, h, 0], st) + ht_ref[d, h, 0]


def rwkv_scan_pallas(lw, r, kk, be, kd, v):
    _, H, T, N = lw.shape
    L = RWKV_CHUNK
    nc = T // L
    in_spec = pl.BlockSpec((1, H, L, N), lambda d, c: (d, 0, c, 0))
    map_spec = pl.BlockSpec((1, H, 1, N, N), lambda d, c: (d, 0, c, 0, 0))
    map_shape = jax.ShapeDtypeStruct((2, H, nc, N, N), jnp.float32)
    gt, ht, rh, y0 = pl.pallas_call(
        _rwkv_prep_kernel,
        grid=(2, nc),
        in_specs=[in_spec] * 6,
        out_specs=[map_spec] * 4,
        out_shape=[map_shape] * 4,
        compiler_params=pltpu.CompilerParams(dimension_semantics=("parallel", "parallel"),
                                             vmem_limit_bytes=VMEM_LIMIT_BYTES),
    )(lw, r, kk, be, kd, v)
    scan_spec = pl.BlockSpec((2, H, 1, N, N), lambda c: (0, 0, c, 0, 0))
    return pl.pallas_call(
        _rwkv_scan_kernel,
        grid=(nc,),
        in_specs=[scan_spec] * 4,
        out_specs=pl.BlockSpec((2, H, L, N), lambda c: (0, 0, c, 0)),
        out_shape=jax.ShapeDtypeStruct((2, H, T, N), jnp.float32),
        scratch_shapes=[pltpu.VMEM((2, H, N, N), jnp.float32)],
        compiler_params=pltpu.CompilerParams(dimension_semantics=("arbitrary",),
                                             vmem_limit_bytes=VMEM_LIMIT_BYTES),
    )(gt, ht, rh, y0)


def rwkv_branch(p_c, p_l, mu_prev, mu_next, w0, w2, a0, a2, g2, k_k, k_a, r_k, lnx_w, lnx_b, need_ctx):
    f32 = jnp.float32
    W = BRANCH_WIDTH
    assert p_l.shape[0] == 1
    C = p_c.shape[1]

    def features(p):
        B, T = p.shape[:2]
        hd = lambda t: t.reshape(B, T, RWKV_HEADS, HEAD_DIM)
        pm = shift_mix(p.astype(f32), mu_prev, mu_next)
        r, k, v, w1f, w1b, a1f, a1b, g1 = split_cols(pm, RWKV_COLS)
        kk = hd(k * k_k)
        kk = kk * lax.rsqrt(jnp.maximum(jnp.sum(kk * kk, axis=-1, keepdims=True), 1e-12))
        dirs = []
        for d, (w1, a1) in enumerate(((w1f, a1f), (w1b, a1b))):
            w = -jax.nn.softplus(-(w0[d] + jnp.tanh(w1) @ w2[d])) - 0.5
            log_decay = -jnp.exp(w)
            a = jax.nn.sigmoid(a0[d] + a1 @ a2[d])
            k_d = k * (1.0 + (a - 1.0) * k_a)
            dirs.append((hd(log_decay), hd(k_d), hd(a)))
        g = jax.nn.sigmoid(g1) @ g2
        return hd(r), hd(v), kk, dirs, g

    def readout(y, r, v, dirs, g, dtype):
        B, T = y.shape[:2]
        mu = jnp.mean(y, axis=-1, keepdims=True)
        var = jnp.mean(jnp.square(y - mu), axis=-1, keepdims=True)
        yn = ((y - mu) * lax.rsqrt(var + RWKV_GN_EPS)).reshape(B, T, W) * lnx_w + lnx_b
        bonus = jnp.sum(r * (dirs[0][1] + dirs[1][1]) * r_k, axis=-1, keepdims=True) * v
        return ((yn + bonus.reshape(B, T, W)) * g).astype(dtype)

    rc, vc, kkc, dirs_c, gc = features(p_c)
    rl, vl, kkl, dirs_l, gl = features(p_l)

    def order(tc, tl):
        fwd = jnp.concatenate([tc[0], tl[0]], axis=0)
        bwd = jnp.concatenate([tc[0, ::-1], tl[0, ::-1]], axis=0)
        return jnp.stack([fwd, bwd]).transpose(0, 2, 1, 3)

    def order2(tc0, tl0, tc1, tl1):
        fwd = jnp.concatenate([tc0[0], tl0[0]], axis=0)
        bwd = jnp.concatenate([tc1[0, ::-1], tl1[0, ::-1]], axis=0)
        return jnp.stack([fwd, bwd]).transpose(0, 2, 1, 3)

    lw = order2(dirs_c[0][0], dirs_l[0][0], dirs_c[1][0], dirs_l[1][0])
    kd = order2(dirs_c[0][1], dirs_l[0][1], dirs_c[1][1], dirs_l[1][1])
    be = order2(kkc * dirs_c[0][2], kkl * dirs_l[0][2], kkc * dirs_c[1][2], kkl * dirs_l[1][2])
    y = rwkv_scan_pallas(lw, order(rc, rl), order(kkc, kkl), be, kd, order(vc, vl))
    y = y.transpose(0, 2, 1, 3)
    y_c = (y[0, :C] + y[1, :C][::-1])[None]
    y_l = (y[0, C:] + y[1, C:][::-1])[None]
    lat = readout(y_l, rl, vl, dirs_l, gl, p_l.dtype)
    ctx_out = readout(y_c, rc, vc, dirs_c, gc, p_c.dtype) if need_ctx else None
    return ctx_out, lat


HGRN_SUB = 16
HGRN_BLOCK = 128


def _hgrn_kernel(q_ref, lf_ref, k_ref, v_ref, seg_ref, o_ref, st_ref):
    TB, L = HGRN_BLOCK, HGRN_SUB
    f32 = jnp.float32
    bf16 = jnp.bfloat16

    @pl.when(pl.program_id(1) == 0)
    def _():
        st_ref[...] = jnp.zeros_like(st_ref)

    q = q_ref[0]
    k = k_ref[0]
    v = v_ref[0]
    seg = seg_ref[...]
    pos = lax.broadcasted_iota(jnp.int32, (TB, 1), 0) % L
    b = lf_ref[0]
    s = 1
    while s < L:
        b = b + jnp.where(pos >= s, pltpu.roll(b, s, axis=0), 0.0)
        s *= 2
    acc = jnp.zeros_like(q)
    for j in range(L):
        if j == 0:
            z = q * k
            vj = v
        else:
            keep = pos >= j
            ex = jnp.exp(jnp.where(keep, b - pltpu.roll(b, j, axis=0), 0.0))
            z = jnp.where(keep, q * pltpu.roll(k, j, axis=0) * ex, 0.0)
            vj = pltpu.roll(v, j, axis=0)
        zh = z.astype(bf16)
        zl = (z - zh.astype(f32)).astype(bf16)
        a = jnp.dot(zh, seg, preferred_element_type=f32) + jnp.dot(zl, seg, preferred_element_type=f32)
        acc = acc + a * vj
    o_ref[0] = acc
    eb = jnp.exp(b)
    qs = (q * eb).astype(bf16)
    same_head = seg > 0
    for i in range(TB // L):
        rows = slice(i * L, (i + 1) * L)
        b_end = b[(i + 1) * L - 1:(i + 1) * L, :]
        st = st_ref[...]
        o_ref[0, rows, :] += lax.dot_general(qs[rows], st.astype(bf16), _NT, preferred_element_type=f32)
        kt = (k[rows] * jnp.exp(b_end - b[rows])).astype(bf16)
        upd = lax.dot_general(v[rows].astype(bf16), kt, _TN, preferred_element_type=f32)
        st_ref[...] = st * jnp.exp(b_end) + jnp.where(same_head, upd, 0.0)


def hgrn_scan_pallas(q, lf, k, v):
    _, T, W = q.shape
    TB = HGRN_BLOCK
    head = jnp.arange(W, dtype=jnp.int32) // HEAD_DIM
    seg = (head[:, None] == head[None, :]).astype(jnp.bfloat16)
    spec = pl.BlockSpec((1, TB, W), lambda d, i: (d, i, 0))
    return pl.pallas_call(
        _hgrn_kernel,
        grid=(2, T // TB),
        in_specs=[spec, spec, spec, spec, pl.BlockSpec((W, W), lambda d, i: (0, 0))],
        out_specs=spec,
        out_shape=jax.ShapeDtypeStruct((2, T, W), jnp.float32),
        scratch_shapes=[pltpu.VMEM((W, W), jnp.float32)],
        compiler_params=pltpu.CompilerParams(dimension_semantics=("parallel", "arbitrary"),
                                             vmem_limit_bytes=VMEM_LIMIT_BYTES),
    )(q, lf, k, v, seg)


def hgrn_branch(p_c, p_l, lb, norm_w, need_ctx):
    f32 = jnp.float32
    log_lb, log_ub = jnp.log(lb), jnp.log1p(-lb)

    def features(p):
        B, T = p.shape[:2]
        hd = lambda t: t.reshape(B, T, HGRN_HEADS, HEAD_DIM)
        q, zf, zb, i, g = split_cols(p.astype(f32), HGRN_COLS)
        gates = []
        for z in (zf, zb):
            log_f = jnp.logaddexp(log_lb, log_ub + jax.nn.log_sigmoid(z))
            k = (1.0 - lb) * jax.nn.sigmoid(-z)
            gates.append((hd(log_f), hd(k)))
        return hd(q), hd(i), gates, g

    def readout(o, g, dtype):
        B, T = o.shape[:2]
        return (rms_norm(o, norm_w).reshape(B, T, BRANCH_WIDTH) * jax.nn.silu(g)).astype(dtype)

    qc, ic, gates_c, gc = features(p_c)
    ql, il, gates_l, gl = features(p_l)
    assert p_l.shape[0] == 1
    C = p_c.shape[1]
    W = BRANCH_WIDTH

    def order(tc0, tl0, tc1, tl1):
        fwd = jnp.concatenate([tc0[0], tl0[0]], axis=0)
        bwd = jnp.concatenate([tc1[0, ::-1], tl1[0, ::-1]], axis=0)
        return jnp.stack([fwd, bwd]).reshape(2, -1, W)

    o = hgrn_scan_pallas(order(qc, ql, qc, ql),
                         order(gates_c[0][0], gates_l[0][0], gates_c[1][0], gates_l[1][0]),
                         order(gates_c[0][1], gates_l[0][1], gates_c[1][1], gates_l[1][1]),
                         order(ic, il, ic, il))
    o = o.reshape(2, -1, HGRN_HEADS, HEAD_DIM)
    lat = readout((o[0, C:] + o[1, C:][::-1])[None], gl, p_l.dtype)
    ctx_out = readout((o[0, :C] + o[1, :C][::-1])[None], gc, p_c.dtype) if need_ctx else None
    return ctx_out, lat


def gated_merge(gate_cols, branches, b_merge, w_branch, w_out):
    B, T = gate_cols.shape[:2]
    gates = jax.nn.sigmoid(gate_cols + b_merge).reshape(B, T, N_BRANCH, D_MODEL)
    br = jnp.stack(branches, axis=2)
    proj = jnp.einsum('btnc,ncd->btnd', br, w_branch)
    return jnp.sum(gates * proj, axis=2) @ w_out


MOE_ROWS = 256


def _moe_expert_kernel(be_ref, nb_ref, x_ref, wgu_ref, wd_ref, bgu_ref, bd_ref, sel_ref, y_ref):
    f32 = jnp.float32

    @pl.when(pl.program_id(0) < nb_ref[0])
    def _():
        gu = jnp.dot(x_ref[...], wgu_ref[0], preferred_element_type=f32) + bgu_ref[0]
        gate = jnp.minimum(gu, SWIGLU_LIMIT)
        up = jnp.clip(pltpu.roll(gu, gu.shape[1] - 1, axis=1), -SWIGLU_LIMIT, SWIGLU_LIMIT)
        glu = gate * jax.nn.sigmoid(SWIGLU_ALPHA * gate)
        act2 = ((up + 1.0) * glu).astype(jnp.bfloat16)
        act = jnp.dot(act2, sel_ref[...], preferred_element_type=f32).astype(jnp.bfloat16)
        y_ref[...] = jnp.dot(act, wd_ref[0], preferred_element_type=f32) + bd_ref[0]

    @pl.when(pl.program_id(0) >= nb_ref[0])
    def _():
        y_ref[...] = jnp.zeros_like(y_ref)


def moe_experts_pallas(xb, block_e, n_used, wgu, wd, bgu, bd):
    cap, D = xb.shape
    F2 = wgu.shape[-1]
    F = F2 // 2
    R = MOE_ROWS
    n_blocks = cap // R
    sel = (jnp.arange(F2, dtype=jnp.int32)[:, None] == 2 * jnp.arange(F, dtype=jnp.int32)[None, :]).astype(jnp.bfloat16)
    grid_spec = pltpu.PrefetchScalarGridSpec(
        num_scalar_prefetch=2,
        grid=(n_blocks,),
        in_specs=[pl.BlockSpec((R, D), lambda i, be, nb: (i, 0)),
                  pl.BlockSpec((1, D, F2), lambda i, be, nb: (be[i], 0, 0)),
                  pl.BlockSpec((1, F, D), lambda i, be, nb: (be[i], 0, 0)),
                  pl.BlockSpec((1, 1, F2), lambda i, be, nb: (be[i], 0, 0)),
                  pl.BlockSpec((1, 1, D), lambda i, be, nb: (be[i], 0, 0)),
                  pl.BlockSpec((F2, F), lambda i, be, nb: (0, 0))],
        out_specs=pl.BlockSpec((R, D), lambda i, be, nb: (i, 0)),
    )
    return pl.pallas_call(
        _moe_expert_kernel,
        grid_spec=grid_spec,
        out_shape=jax.ShapeDtypeStruct((cap, D), jnp.float32),
        compiler_params=pltpu.CompilerParams(dimension_semantics=("arbitrary",),
                                             vmem_limit_bytes=MOE_VMEM_LIMIT_BYTES),
    )(block_e, n_used, xb, wgu, wd, bgu, bd, sel)


def moe_ffn(h, router_w, router_b, gu_w, gu_b, dn_w, dn_b):
    N, D = h.shape
    R = MOE_ROWS
    logits = (jnp.dot(h, router_w, precision=lax.Precision.HIGHEST) + router_b).astype(jnp.float32)
    top_l, top_e = lax.top_k(logits, TOP_K)
    weights = jax.nn.softmax(top_l, axis=-1)
    n_assign = N * TOP_K
    flat_e = top_e.reshape(-1)
    order = jnp.argsort(flat_e)
    e_sorted = flat_e[order]
    tok_sorted = (jnp.arange(n_assign, dtype=jnp.int32) // TOP_K)[order]
    counts = jnp.bincount(flat_e, length=N_EXPERTS)
    padded = (counts + R - 1) // R * R
    start_sorted = jnp.cumsum(counts) - counts
    pad_end = jnp.cumsum(padded)
    start_pad = pad_end - padded
    dest = (start_pad[e_sorted] + jnp.arange(n_assign, dtype=jnp.int32) - start_sorted[e_sorted]).astype(jnp.int32)
    n_blocks = -(-(n_assign + N_EXPERTS * (R - 1)) // R)
    cap = n_blocks * R
    slot_tok = jnp.full((cap,), N, dtype=jnp.int32).at[dest].set(tok_sorted)
    slot_of_assign = jnp.zeros((n_assign,), jnp.int32).at[order].set(dest)
    block_e = jnp.minimum(jnp.searchsorted(pad_end, jnp.arange(n_blocks, dtype=jnp.int32) * R, side='right'),
                          N_EXPERTS - 1).astype(jnp.int32)
    n_used = (pad_end[-1] // R).astype(jnp.int32).reshape(1)
    h_pad = jnp.concatenate([h.astype(jnp.bfloat16), jnp.zeros((1, D), jnp.bfloat16)], axis=0)
    xb = h_pad[slot_tok]
    bf16 = jnp.bfloat16
    yb = moe_experts_pallas(xb, block_e, n_used, gu_w.astype(bf16), dn_w.astype(bf16),
                            gu_b[:, None, :], dn_b[:, None, :])
    picked = yb[slot_of_assign].reshape(N, TOP_K, D)
    return jnp.sum(picked * weights.astype(h.dtype)[:, :, None], axis=1)


def _mm_kernel(a_ref, b_ref, o_ref):
    o_ref[...] = jnp.dot(a_ref[...], b_ref[...], preferred_element_type=jnp.float32)


def _pick(n, cands):
    for c in cands:
        if n % c == 0:
            return c
    return n


def pallas_matmul(a, b):
    M, K = a.shape
    N = b.shape[1]
    tm = _pick(M, (1056, 1024, 768, 512, 256, 128))
    tn = _pick(N, (1152, 1024, 640, 512, 256, 128))
    return pl.pallas_call(
        _mm_kernel,
        grid=(M // tm, N // tn),
        in_specs=[pl.BlockSpec((tm, K), lambda i, j: (i, 0)),
                  pl.BlockSpec((K, tn), lambda i, j: (0, j))],
        out_specs=pl.BlockSpec((tm, tn), lambda i, j: (i, j)),
        out_shape=jax.ShapeDtypeStruct((M, N), jnp.float32),
        compiler_params=pltpu.CompilerParams(dimension_semantics=("parallel", "parallel"),
                                             vmem_limit_bytes=VMEM_LIMIT_BYTES),
    )(a, b)


def kernel(x, c, ctx, c_ctx, ada_w, ada_b, norm_mix, norm_ffn, w_in, b_merge, qk_norm_q, qk_norm_k, att_sink,
           ssm_lam_re, ssm_lam_im, ssm_log_dt, ssm_b_re, ssm_b_im, ssm_c_re, ssm_c_im, ssm_d, ssm_glu_w, ssm_glu_b,
           rwkv_mu_prev, rwkv_mu_next, rwkv_w0, rwkv_w2, rwkv_a0, rwkv_a2, rwkv_g2, rwkv_k_k, rwkv_k_a, rwkv_r_k,
           rwkv_lnx_w, rwkv_lnx_b, hgrn_lb_raw, hgrn_norm, w_branch, w_out, router_w, router_b,
           exp_gu_w, exp_gu_b, exp_down_w, exp_down_b):
    B, S, D = x.shape
    C = ctx.shape[1]
    rows = S // GRID_W
    cos, sin = axial_rope(rows)
    lb_all = jnp.cumsum(jax.nn.softmax(hgrn_lb_raw.astype(jnp.float32), axis=0), axis=0)
    lb_all = lb_all - lb_all[:1]
    h_ctx = ctx
    for l in range(DEPTH):
        need_ctx = l < DEPTH - 1
        m_lat = [t[:, None, :] for t in jnp.split(jax.nn.silu(c) @ ada_w[l] + ada_b[l], 6, axis=-1)]
        m_ctx = jnp.split(jax.nn.silu(c_ctx) @ ada_w[l] + ada_b[l], 6, axis=-1)
        hl = modulate(rms_norm(x, norm_mix[l]), m_lat[0], m_lat[1])
        hc = modulate(rms_norm(h_ctx, norm_mix[l]), m_ctx[0], m_ctx[1])
        h_all = jnp.concatenate([hc[0], hl[0]], axis=0).astype(jnp.bfloat16)
        n_mix = D_IN - N_BRANCH * D_MODEL
        p_mix = pallas_matmul(h_all, w_in[l][:, :n_mix].astype(jnp.bfloat16))
        p_gate = pallas_matmul(h_all, w_in[l][:, n_mix:].astype(jnp.bfloat16))
        mix_cols = IN_COLS[:-1]
        qc_, kc_, vc_, uc, rwc, hgc = split_cols(p_mix[:C][None], mix_cols)
        ql_, kl_, vl_, ul, rwl, hgl = split_cols(p_mix[C:][None], mix_cols)
        gtc, gtl = p_gate[:C][None], p_gate[C:][None]
        att_c, att_l = attention_branch(qc_, kc_, vc_, ql_, kl_, vl_, qk_norm_q[l], qk_norm_k[l], att_sink[l],
                                        cos, sin, need_ctx)
        ssm_c, ssm_l = s5_branch(uc, ul, ssm_lam_re[l], ssm_lam_im[l], ssm_log_dt[l], ssm_b_re[l], ssm_b_im[l],
                                 ssm_c_re[l], ssm_c_im[l], ssm_d[l], ssm_glu_w[l], ssm_glu_b[l], need_ctx)
        rw_c, rw_l = rwkv_branch(rwc, rwl, rwkv_mu_prev[l], rwkv_mu_next[l], rwkv_w0[l], rwkv_w2[l], rwkv_a0[l],
                                 rwkv_a2[l], rwkv_g2[l], rwkv_k_k[l], rwkv_k_a[l], rwkv_r_k[l], rwkv_lnx_w[l],
                                 rwkv_lnx_b[l], need_ctx)
        hg_c, hg_l = hgrn_branch(hgc, hgl, lb_all[l], hgrn_norm[l], need_ctx)
        x = x + m_lat[2] * gated_merge(gtl, (att_l, ssm_l, rw_l, hg_l), b_merge[l], w_branch[l], w_out[l])
        hl2 = modulate(rms_norm(x, norm_ffn[l]), m_lat[3], m_lat[4])
        moe_args = (router_w[l], router_b[l], exp_gu_w[l], exp_gu_b[l], exp_down_w[l], exp_down_b[l])
        if need_ctx:
            h_ctx = h_ctx + m_ctx[2] * gated_merge(gtc, (att_c, ssm_c, rw_c, hg_c), b_merge[l], w_branch[l], w_out[l])
            hc2 = modulate(rms_norm(h_ctx, norm_ffn[l]), m_ctx[3], m_ctx[4])
            tokens = jnp.concatenate([hc2.reshape(B * C, D), hl2.reshape(B * S, D)], axis=0)
            y = moe_ffn(tokens, *moe_args)
            h_ctx = h_ctx + m_ctx[5] * y[:B * C].reshape(B, C, D)
            x = x + m_lat[5] * y[B * C:].reshape(B, S, D)
        else:
            y = moe_ffn(hl2.reshape(B * S, D), *moe_args)
            x = x + m_lat[5] * y.reshape(B, S, D)
    return x
```

```python
import functools
import math
import jax
import jax.numpy as jnp
from jax import lax
import numpy as np
from jax.experimental import pallas as pl
from jax.experimental.pallas import tpu as pltpu

D_MODEL = 2048
BATCH = 1
SEQ = 8192
DEPTH = 2

GRID_W = 64
CTX_LEN = 256
HEAD_DIM = 64
N_BRANCH = 4
BRANCH_WIDTH = D_MODEL // 4

ATT_HEADS = BRANCH_WIDTH // HEAD_DIM
ATT_KV_HEADS = ATT_HEADS // 4
ATT_GROUP = ATT_HEADS // ATT_KV_HEADS
WINDOW = 128
ATT_BLOCK = 128
ROPE_BASE = 10000.0

SSM_GROUP = 16
SSM_GROUPS = BRANCH_WIDTH // SSM_GROUP
SSM_STATE = 64

RWKV_HEADS = BRANCH_WIDTH // HEAD_DIM
DECAY_LORA = 64
ICLR_LORA = 64
GATE_LORA = 128
RWKV_GN_EPS = 64e-5

HGRN_HEADS = BRANCH_WIDTH // HEAD_DIM
HGRN_CHUNK = 64

N_EXPERTS = 32
TOP_K = 4
D_EXPERT = D_MODEL // 2
SWIGLU_LIMIT = 7.0
SWIGLU_ALPHA = 1.702
MOE_BLOCK = 128

NORM_EPS = 1e-6

ATT_COLS = (ATT_HEADS * HEAD_DIM, ATT_KV_HEADS * HEAD_DIM, ATT_KV_HEADS * HEAD_DIM)
RWKV_COLS = (BRANCH_WIDTH, BRANCH_WIDTH, BRANCH_WIDTH, DECAY_LORA, DECAY_LORA, ICLR_LORA, ICLR_LORA, GATE_LORA)
HGRN_COLS = (BRANCH_WIDTH,) * 5
IN_COLS = ATT_COLS + (BRANCH_WIDTH, sum(RWKV_COLS), sum(HGRN_COLS), N_BRANCH * D_MODEL)
D_IN = sum(IN_COLS)

VMEM_LIMIT_BYTES = 48 * 1024 * 1024
MOE_VMEM_LIMIT_BYTES = 56 * 1024 * 1024


def split_cols(p, sizes):
    idx = [sum(sizes[:i + 1]) for i in range(len(sizes) - 1)]
    return jnp.split(p, idx, axis=-1)


def rms_norm(x, w):
    xf = x.astype(jnp.float32)
    y = xf * lax.rsqrt(jnp.mean(xf * xf, axis=-1, keepdims=True) + NORM_EPS)
    return (y * w).astype(x.dtype)


def modulate(h, shift, scale):
    return h * (1.0 + scale) + shift


def axial_rope(rows):
    n_freq = HEAD_DIM // 4
    inv = ROPE_BASE ** (-jnp.arange(n_freq, dtype=jnp.float32) / n_freq)
    row = jnp.repeat(jnp.arange(rows, dtype=jnp.float32), GRID_W)
    col = jnp.tile(jnp.arange(GRID_W, dtype=jnp.float32), rows)
    ang = jnp.concatenate([row[:, None] * inv, col[:, None] * inv], axis=-1)
    return jnp.cos(ang), jnp.sin(ang)


def apply_rope(x, cos, sin):
    xf = x.astype(jnp.float32)
    half = HEAD_DIM // 2
    x1, x2 = xf[..., :half], xf[..., half:]
    cs, sn = cos[None, :, None, :], sin[None, :, None, :]
    return jnp.concatenate([x1 * cs - x2 * sn, x1 * sn + x2 * cs], axis=-1).astype(x.dtype)


def softmax_with_sink(logits, sink):
    m = jnp.maximum(jnp.max(logits, axis=-1, keepdims=True), sink)
    e = jnp.exp(logits - m)
    return e / (jnp.sum(e, axis=-1, keepdims=True) + jnp.exp(sink - m))


def attention_branch(q_c, k_c, v_c, q_l, k_l, v_l, qn_w, kn_w, sink, cos, sin, need_ctx):
    B, S = q_l.shape[:2]
    C = q_c.shape[1]
    heads = lambda t, h: t.reshape(t.shape[0], t.shape[1], h, HEAD_DIM)
    kc = rms_norm(heads(k_c, ATT_KV_HEADS), kn_w)
    vc = heads(v_c, ATT_KV_HEADS)
    ql = apply_rope(rms_norm(heads(q_l, ATT_HEADS), qn_w), cos, sin)
    kl = apply_rope(rms_norm(heads(k_l, ATT_KV_HEADS), kn_w), cos, sin)
    vl = heads(v_l, ATT_KV_HEADS)
    scale = HEAD_DIM ** -0.5
    sink_f = sink.astype(jnp.float32).reshape(ATT_KV_HEADS, ATT_GROUP)
    nb = S // ATT_BLOCK
    qb = ql.reshape(B, nb, ATT_BLOCK, ATT_KV_HEADS, ATT_GROUP, HEAD_DIM)

    def band(t):
        tp = jnp.pad(t, ((0, 0), (ATT_BLOCK, ATT_BLOCK), (0, 0), (0, 0)))
        tp = tp.reshape(B, nb + 2, ATT_BLOCK, ATT_KV_HEADS, HEAD_DIM)
        return jnp.concatenate([tp[:, :-2], tp[:, 1:-1], tp[:, 2:]], axis=2)

    kb, vb = band(kl), band(vl)
    s_win = jnp.einsum('bnqhgd,bnkhd->bnhgqk', qb, kb, preferred_element_type=jnp.float32) * scale
    q_pos = jnp.arange(nb)[:, None] * ATT_BLOCK + jnp.arange(ATT_BLOCK)[None, :]
    k_pos = jnp.arange(nb)[:, None] * ATT_BLOCK + jnp.arange(3 * ATT_BLOCK)[None, :] - ATT_BLOCK
    kp = k_pos[:, None, :]
    valid = (jnp.abs(kp - q_pos[:, :, None]) <= WINDOW) & (kp >= 0) & (kp < S)
    s_win = jnp.where(valid[None, :, None, None], s_win, -jnp.inf)
    s_ctx = jnp.einsum('bnqhgd,bkhd->bnhgqk', qb, kc, preferred_element_type=jnp.float32) * scale
    p = softmax_with_sink(jnp.concatenate([s_win, s_ctx], axis=-1), sink_f[None, None, :, :, None, None])
    p = p.astype(vl.dtype)
    o = (jnp.einsum('bnhgqk,bnkhd->bnqhgd', p[..., :3 * ATT_BLOCK], vb)
         + jnp.einsum('bnhgqk,bkhd->bnqhgd', p[..., 3 * ATT_BLOCK:], vc))
    lat = o.reshape(B, S, ATT_HEADS * HEAD_DIM)
    ctx_out = None
    if need_ctx:
        qc = rms_norm(heads(q_c, ATT_HEADS), qn_w).reshape(B, C, ATT_KV_HEADS, ATT_GROUP, HEAD_DIM)
        s_cc = jnp.einsum('bqhgd,bkhd->bhgqk', qc, kc, preferred_element_type=jnp.float32) * scale
        pc = softmax_with_sink(s_cc, sink_f[None, :, :, None, None]).astype(vc.dtype)
        ctx_out = jnp.einsum('bhgqk,bkhd->bqhgd', pc, vc).reshape(B, C, ATT_HEADS * HEAD_DIM)
    return ctx_out, lat


S5_BLOCK = 128


def _s5_scan_kernel(reverse, u_ref, bre_ref, bim_ref, pwr_ref, pwi_ref, cre_ref, cim_ref, y_ref, car_ref):
    TB = S5_BLOCK
    f32 = jnp.float32

    @pl.when(pl.program_id(0) == 0)
    def _():
        car_ref[...] = jnp.zeros_like(car_ref)

    u = u_ref[...].astype(jnp.bfloat16)
    xr = jnp.dot(u, bre_ref[...], preferred_element_type=f32)
    xi = jnp.dot(u, bim_ref[...], preferred_element_type=f32)
    rows = lax.broadcasted_iota(jnp.int32, (TB, 1), 0)
    s = 1
    while s < TB:
        p = TB - s if reverse else s - 1
        ar = pwr_ref[p:p + 1, :]
        ai = pwi_ref[p:p + 1, :]
        keep = rows < TB - s if reverse else rows >= s
        shift = TB - s if reverse else s
        sr = jnp.where(keep, pltpu.roll(xr, shift, axis=0), 0.0)
        si = jnp.where(keep, pltpu.roll(xi, shift, axis=0), 0.0)
        xr, xi = xr + (ar * sr - ai * si), xi + (ar * si + ai * sr)
        s *= 2
    cr = car_ref[0:1, :]
    ci = car_ref[1:2, :]
    pr = pwr_ref[...]
    pi_ = pwi_ref[...]
    xr = xr + (pr * cr - pi_ * ci)
    xi = xi + (pr * ci + pi_ * cr)
    last = 0 if reverse else TB - 1
    car_ref[0:1, :] = xr[last:last + 1, :]
    car_ref[1:2, :] = xi[last:last + 1, :]
    y_ref[...] = (jnp.dot(xr.astype(jnp.bfloat16), cre_ref[...], preferred_element_type=f32)
                  - jnp.dot(xi.astype(jnp.bfloat16), cim_ref[...], preferred_element_type=f32))


def s5_scan_pallas(u, bt_re, bt_im, pw_re, pw_im, c_re, c_im, reverse, n_ctx_blocks):
    T, W = u.shape
    N = bt_re.shape[-1]
    TB = S5_BLOCK
    nb = T // TB
    blk = (lambda i: (_backward_block(i, n_ctx_blocks, nb), 0)) if reverse else (lambda i: (i, 0))
    const = lambda i: (0, 0)
    return pl.pallas_call(
        functools.partial(_s5_scan_kernel, reverse),
        grid=(nb,),
        in_specs=[pl.BlockSpec((TB, W), blk),
                  pl.BlockSpec((W, N), const),
                  pl.BlockSpec((W, N), const),
                  pl.BlockSpec((TB, N), const),
                  pl.BlockSpec((TB, N), const),
                  pl.BlockSpec((N, W), const),
                  pl.BlockSpec((N, W), const)],
        out_specs=pl.BlockSpec((TB, W), blk),
        out_shape=jax.ShapeDtypeStruct((T, W), jnp.float32),
        scratch_shapes=[pltpu.VMEM((2, N), jnp.float32)],
        compiler_params=pltpu.CompilerParams(dimension_semantics=("arbitrary",),
                                             vmem_limit_bytes=VMEM_LIMIT_BYTES),
    )(u, bt_re, bt_im, pw_re, pw_im, c_re, c_im)


def s5_branch(u_c, u_l, lam_re, lam_im, log_dt, b_re, b_im, c_re, c_im, d_skip, glu_w, glu_b, need_ctx):
    f32 = jnp.float32
    assert u_l.shape[0] == 1
    C = u_c.shape[1]
    G, P, GC = SSM_GROUPS, SSM_STATE, SSM_GROUP
    dt = jnp.exp(log_dt.astype(f32))[..., None]
    lr, li = lam_re.astype(f32), lam_im.astype(f32)
    mag = jnp.exp(dt * lr)
    ab_re, ab_im = mag * jnp.cos(dt * li), mag * jnp.sin(dt * li)
    den = lr * lr + li * li
    gm_re = ((ab_re - 1.0) * lr + ab_im * li) / den
    gm_im = (ab_im * lr - (ab_re - 1.0) * li) / den
    eye_g = jnp.eye(G, dtype=f32)
    br = jnp.transpose(b_re.astype(f32), (0, 2, 1))
    bi = jnp.transpose(b_im.astype(f32), (0, 2, 1))
    gr, gi = gm_re[:, :, None, :], gm_im[:, :, None, :]
    blockdiag = lambda m: jnp.einsum('dgcp,gh->dgchp', m, eye_g).reshape(2, G * GC, G * P)
    bt_re = blockdiag(gr * br - gi * bi).astype(jnp.bfloat16)
    bt_im = blockdiag(gr * bi + gi * br).astype(jnp.bfloat16)
    n = jnp.arange(1, S5_BLOCK + 1, dtype=f32)[None, :, None, None]
    theta = dt * li
    theta = theta - (2.0 * math.pi) * jnp.round(theta / (2.0 * math.pi))
    pmag = jnp.exp(n * (dt * lr)[:, None])
    pw_re = (pmag * jnp.cos(n * theta[:, None])).reshape(2, S5_BLOCK, G * P)
    pw_im = (pmag * jnp.sin(n * theta[:, None])).reshape(2, S5_BLOCK, G * P)
    cbd = lambda m: jnp.einsum('gcp,gh->gphc', m.astype(f32), eye_g).reshape(G * P, G * GC).astype(jnp.bfloat16)
    uc, ul = u_c[0].astype(f32), u_l[0].astype(f32)
    u_all = jnp.concatenate([uc, ul], axis=0)
    cr_bd, ci_bd = cbd(c_re), cbd(c_im)
    yp = (s5_scan_pallas(u_all, bt_re[0], bt_im[0], pw_re[0], pw_im[0], cr_bd, ci_bd, False, C // S5_BLOCK)
          + s5_scan_pallas(u_all, bt_re[1], bt_im[1], pw_re[1, ::-1], pw_im[1, ::-1], cr_bd, ci_bd, True,
                           C // S5_BLOCK))

    def readout(y, u):
        y = y + d_skip * u.astype(f32)
        z = jax.nn.gelu(y) @ glu_w + glu_b
        za, zb = jnp.split(z, 2, axis=-1)
        return (za * jax.nn.sigmoid(zb)).astype(u.dtype)

    lat = readout(yp[C:][None], u_l)
    ctx_out = readout(yp[:C][None], u_c) if need_ctx else None
    return ctx_out, lat


def shift_mix(p, mu_prev, mu_next):
    zero = jnp.zeros_like(p[:, :1])
    prev = jnp.concatenate([zero, p[:, :-1]], axis=1)
    nxt = jnp.concatenate([p[:, 1:], zero], axis=1)
    return p + mu_prev * (prev - p) + mu_next * (nxt - p)


RWKV_CHUNK = 64
RWKV_INV_BLOCK = 16

_NN = (((1,), (0,)), ((), ()))
_NT = (((1,), (1,)), ((), ()))
_TN = (((0,), (0,)), ((), ()))


def _split_bf16(x):
    hi = x.astype(jnp.bfloat16)
    lo = (x - hi.astype(jnp.float32)).astype(jnp.bfloat16)
    return hi, lo


def _dot3(a, b, dims=_NN):
    ah, al = _split_bf16(a)
    bh, bl = _split_bf16(b)
    d = functools.partial(lax.dot_general, dimension_numbers=dims, preferred_element_type=jnp.float32)
    return d(ah, bh) + (d(ah, bl) + d(al, bh))


def _each(fn, *lists):
    return [fn(*args) for args in zip(*lists)]


def _rwkv_chunk_maps(lw, r, kk, be, kd, v, rev):
    L = RWKV_CHUNK
    f32 = jnp.float32
    row = lax.broadcasted_iota(jnp.int32, (L, L), 0)
    col = lax.broadcasted_iota(jnp.int32, (L, L), 1)
    ahead = jnp.where(rev, col - row, row - col)
    tril_i = ahead >= 0
    tril_s = ahead > 0
    blk = (row // RWKV_INV_BLOCK) == (col // RWKV_INV_BLOCK)
    eye = jnp.where(row == col, 1.0, 0.0).astype(f32)
    tri = jnp.where(tril_i, 1.0, 0.0).astype(jnp.bfloat16)
    dd = functools.partial(jnp.dot, preferred_element_type=f32)

    def cumsum(x):
        h1 = x.astype(jnp.bfloat16)
        r1 = x - h1.astype(f32)
        h2 = r1.astype(jnp.bfloat16)
        h3 = (r1 - h2.astype(f32)).astype(jnp.bfloat16)
        return dd(tri, h1) + (dd(tri, h2) + dd(tri, h3))

    c = _each(cumsum, lw)
    c_end = [jnp.where(rev, x[0:1, :], x[L - 1:L, :]) for x in c]
    e_c = [jnp.exp(x) for x in c]
    e_nc = [jnp.exp(-x) for x in c]
    e_end = [jnp.exp(x) for x in c_end]
    al = _each(lambda a, x, w: a * jnp.exp(x - w), kk, c, lw)
    rt = _each(lambda a, e: a * e, r, e_c)
    bb = _each(lambda a, e: a * e, be, e_nc)
    kb = _each(lambda a, e: a * e, kd, e_nc)
    bh = _each(lambda a, e: a * e, bb, e_end)
    kh = _each(lambda a, e: a * e, kb, e_end)
    lhs = _each(lambda a, b: jnp.concatenate([a, b], axis=0), al, rt)
    g1 = _each(lambda a, b: _dot3(a, b, _NT), lhs, bb)
    g2 = _each(lambda a, b: _dot3(a, b, _NT), lhs, kb)
    n = [jnp.where(tril_s, g[:L], 0.0) for g in g1]
    arb = [jnp.where(tril_i, g[L:], 0.0) for g in g1]
    aak = [jnp.where(tril_s, g[:L], 0.0) for g in g2]
    ark = [jnp.where(tril_i, g[L:], 0.0) for g in g2]
    d1 = [jnp.where(blk, x, 0.0) for x in n]
    o1 = _each(lambda a, b: a - b, n, d1)
    d2 = _each(_dot3, d1, d1)
    d4 = _each(_dot3, d2, d2)
    d8 = _each(_dot3, d4, d4)
    imd = [eye - x for x in d1]
    t1 = _each(lambda a, b: a + _dot3(a, b), imd, d2)
    t2 = _each(lambda a, b: a + _dot3(a, b), t1, d4)
    dinv = _each(lambda a, b: a + _dot3(a, b), t2, d8)
    e1 = _each(_dot3, dinv, o1)
    e2 = _each(_dot3, e1, e1)
    ime = [eye - x for x in e1]
    fm = _each(lambda a, b: a + _dot3(a, b), ime, e2)
    minv = _each(_dot3, fm, dinv)
    at = _each(_dot3, minv, al)
    av = _each(_dot3, aak, v)
    pm = _each(_dot3, minv, av)
    rh = _each(lambda a, b, x: a - _dot3(b, x), rt, arb, at)
    y0 = _each(lambda a, x, b, p: _dot3(a, x) - _dot3(b, p), ark, v, arb, pm)
    gt = _each(lambda e, b, a: eye * e - _dot3(b, a, _TN), e_end, bh, at)
    ht = _each(lambda k, x, b, p: _dot3(k, x, _TN) - _dot3(b, p, _TN), kh, v, bh, pm)
    return gt, ht, rh, y0


def _rwkv_prep_kernel(lw_ref, be_ref, kd_ref, r_ref, kk_ref, v_ref, gt_ref, ht_ref, rh_ref, y0_ref):
    rev = pl.program_id(0) == 1
    hs = range(RWKV_HEADS)
    gt, ht, rh, y0 = _rwkv_chunk_maps([lw_ref[0, h] for h in hs], [r_ref[h] for h in hs], [kk_ref[h] for h in hs],
                                      [be_ref[0, h] for h in hs], [kd_ref[0, h] for h in hs],
                                      [v_ref[h] for h in hs], rev)
    for h in hs:
        gt_ref[0, h, 0] = gt[h]
        ht_ref[0, h, 0] = ht[h]
        rh_ref[0, h, 0] = rh[h]
        y0_ref[0, h, 0] = y0[h]


def _rwkv_scan_kernel(gt_ref, ht_ref, rh_ref, y0_ref, yf_ref, yb_ref, st_ref):
    @pl.when(pl.program_id(0) == 0)
    def _():
        st_ref[...] = jnp.zeros_like(st_ref)

    for d, y_ref in enumerate((yf_ref, yb_ref)):
        for h in range(RWKV_HEADS):
            st = st_ref[d, h]
            y_ref[h] = _dot3(rh_ref[d, h, 0], st) + y0_ref[d, h, 0]
            st_ref[d, h] = _dot3(gt_ref[d, h, 0], st) + ht_ref[d, h, 0]


def _backward_block(i, n_ctx, n_all):
    return jnp.where(i < n_ctx, n_ctx - 1 - i, n_all - 1 - (i - n_ctx))


def rwkv_scan_pallas(lw, be, kd, r, kk, v, n_ctx_chunks):
    _, H, T, N = lw.shape
    L = RWKV_CHUNK
    nc = T // L
    chunk = lambda d, i: jnp.where(d == 0, i, _backward_block(i, n_ctx_chunks, nc))
    dir_spec = pl.BlockSpec((1, H, L, N), lambda d, i: (d, 0, chunk(d, i), 0))
    shared_spec = pl.BlockSpec((H, L, N), lambda d, i: (0, chunk(d, i), 0))
    map_spec = pl.BlockSpec((1, H, 1, N, N), lambda d, i: (d, 0, i, 0, 0))
    map_shape = jax.ShapeDtypeStruct((2, H, nc, N, N), jnp.float32)
    gt, ht, rh, y0 = pl.pallas_call(
        _rwkv_prep_kernel,
        grid=(2, nc),
        in_specs=[dir_spec] * 3 + [shared_spec] * 3,
        out_specs=[map_spec] * 4,
        out_shape=[map_shape] * 4,
        compiler_params=pltpu.CompilerParams(dimension_semantics=("parallel", "parallel"),
                                             vmem_limit_bytes=VMEM_LIMIT_BYTES),
    )(lw, be, kd, r, kk, v)
    scan_spec = pl.BlockSpec((2, H, 1, N, N), lambda i: (0, 0, i, 0, 0))
    y_shape = jax.ShapeDtypeStruct((H, T, N), jnp.float32)
    return pl.pallas_call(
        _rwkv_scan_kernel,
        grid=(nc,),
        in_specs=[scan_spec] * 4,
        out_specs=[pl.BlockSpec((H, L, N), lambda i: (0, i, 0)),
                   pl.BlockSpec((H, L, N), lambda i: (0, _backward_block(i, n_ctx_chunks, nc), 0))],
        out_shape=[y_shape, y_shape],
        scratch_shapes=[pltpu.VMEM((2, H, N, N), jnp.float32)],
        compiler_params=pltpu.CompilerParams(dimension_semantics=("arbitrary",),
                                             vmem_limit_bytes=VMEM_LIMIT_BYTES),
    )(gt, ht, rh, y0)


def rwkv_branch(p_c, p_l, mu_prev, mu_next, w0, w2, a0, a2, g2, k_k, k_a, r_k, lnx_w, lnx_b, need_ctx):
    f32 = jnp.float32
    W = BRANCH_WIDTH
    assert p_l.shape[0] == 1
    C = p_c.shape[1]

    def features(p):
        B, T = p.shape[:2]
        hd = lambda t: t.reshape(B, T, RWKV_HEADS, HEAD_DIM)
        pm = shift_mix(p.astype(f32), mu_prev, mu_next)
        r, k, v, w1f, w1b, a1f, a1b, g1 = split_cols(pm, RWKV_COLS)
        kk = hd(k * k_k)
        kk = kk * lax.rsqrt(jnp.maximum(jnp.sum(kk * kk, axis=-1, keepdims=True), 1e-12))
        dirs = []
        for d, (w1, a1) in enumerate(((w1f, a1f), (w1b, a1b))):
            w = -jax.nn.softplus(-(w0[d] + jnp.tanh(w1) @ w2[d])) - 0.5
            log_decay = -jnp.exp(w)
            a = jax.nn.sigmoid(a0[d] + a1 @ a2[d])
            k_d = k * (1.0 + (a - 1.0) * k_a)
            dirs.append((hd(log_decay), hd(k_d), hd(a)))
        g = jax.nn.sigmoid(g1) @ g2
        return hd(r), hd(v), kk, dirs, g

    def readout(y, r, v, dirs, g, dtype):
        B, T = y.shape[:2]
        mu = jnp.mean(y, axis=-1, keepdims=True)
        var = jnp.mean(jnp.square(y - mu), axis=-1, keepdims=True)
        yn = ((y - mu) * lax.rsqrt(var + RWKV_GN_EPS)).reshape(B, T, W) * lnx_w + lnx_b
        bonus = jnp.sum(r * (dirs[0][1] + dirs[1][1]) * r_k, axis=-1, keepdims=True) * v
        return ((yn + bonus.reshape(B, T, W)) * g).astype(dtype)

    rc, vc, kkc, dirs_c, gc = features(p_c)
    rl, vl, kkl, dirs_l, gl = features(p_l)

    def nat(tc, tl):
        return jnp.concatenate([tc[0], tl[0]], axis=0).transpose(1, 0, 2)

    both = lambda f: jnp.stack([f(0), f(1)])
    lw = both(lambda d: nat(dirs_c[d][0], dirs_l[d][0]))
    kd = both(lambda d: nat(dirs_c[d][1], dirs_l[d][1]))
    be = both(lambda d: nat(kkc * dirs_c[d][2], kkl * dirs_l[d][2]))
    y_f, y_b = rwkv_scan_pallas(lw, be, kd, nat(rc, rl), nat(kkc, kkl), nat(vc, vl), C // RWKV_CHUNK)
    y = (y_f + y_b).transpose(1, 0, 2)
    lat = readout(y[C:][None], rl, vl, dirs_l, gl, p_l.dtype)
    ctx_out = readout(y[:C][None], rc, vc, dirs_c, gc, p_c.dtype) if need_ctx else None
    return ctx_out, lat


HGRN_SUB = 16
HGRN_BLOCK = 128


def _hgrn_kernel(reverse, q_ref, lf_ref, k_ref, v_ref, seg_ref, o_ref, st_ref):
    TB, L = HGRN_BLOCK, HGRN_SUB
    f32 = jnp.float32
    bf16 = jnp.bfloat16

    @pl.when(pl.program_id(0) == 0)
    def _():
        st_ref[...] = jnp.zeros_like(st_ref)

    q = q_ref[...]
    k = k_ref[...]
    v = v_ref[...]
    seg = seg_ref[...]
    pos = lax.broadcasted_iota(jnp.int32, (TB, 1), 0) % L
    if reverse:
        pos = L - 1 - pos
    back = lambda x, j: pltpu.roll(x, TB - j if reverse else j, axis=0)
    b = lf_ref[...]
    s = 1
    while s < L:
        b = b + jnp.where(pos >= s, back(b, s), 0.0)
        s *= 2
    acc = jnp.zeros_like(q)
    for j in range(L):
        if j == 0:
            z = q * k
            vj = v
        else:
            keep = pos >= j
            ex = jnp.exp(jnp.where(keep, b - back(b, j), 0.0))
            z = jnp.where(keep, q * back(k, j) * ex, 0.0)
            vj = back(v, j)
        zh = z.astype(bf16)
        zl = (z - zh.astype(f32)).astype(bf16)
        a = jnp.dot(zh, seg, preferred_element_type=f32) + jnp.dot(zl, seg, preferred_element_type=f32)
        acc = acc + a * vj
    o_ref[...] = acc
    eb = jnp.exp(b)
    qs = (q * eb).astype(bf16)
    same_head = seg > 0
    n_sub = TB // L
    for step in range(n_sub):
        i = n_sub - 1 - step if reverse else step
        rows = slice(i * L, (i + 1) * L)
        last = i * L if reverse else (i + 1) * L - 1
        b_end = b[last:last + 1, :]
        st = st_ref[...]
        o_ref[rows, :] += lax.dot_general(qs[rows], st.astype(bf16), _NT, preferred_element_type=f32)
        kt = (k[rows] * jnp.exp(b_end - b[rows])).astype(bf16)
        upd = lax.dot_general(v[rows].astype(bf16), kt, _TN, preferred_element_type=f32)
        st_ref[...] = st * jnp.exp(b_end) + jnp.where(same_head, upd, 0.0)


def hgrn_scan_pallas(q, lf, k, v, reverse, n_ctx_blocks):
    T, W = q.shape
    TB = HGRN_BLOCK
    nb = T // TB
    head = jnp.arange(W, dtype=jnp.int32) // HEAD_DIM
    seg = (head[:, None] == head[None, :]).astype(jnp.bfloat16)
    blk = (lambda i: (_backward_block(i, n_ctx_blocks, nb), 0)) if reverse else (lambda i: (i, 0))
    spec = pl.BlockSpec((TB, W), blk)
    return pl.pallas_call(
        functools.partial(_hgrn_kernel, reverse),
        grid=(nb,),
        in_specs=[spec, spec, spec, spec, pl.BlockSpec((W, W), lambda i: (0, 0))],
        out_specs=spec,
        out_shape=jax.ShapeDtypeStruct((T, W), jnp.float32),
        scratch_shapes=[pltpu.VMEM((W, W), jnp.float32)],
        compiler_params=pltpu.CompilerParams(dimension_semantics=("arbitrary",),
                                             vmem_limit_bytes=VMEM_LIMIT_BYTES),
    )(q, lf, k, v, seg)


def hgrn_branch(p_c, p_l, lb, norm_w, need_ctx):
    f32 = jnp.float32
    log_lb, log_ub = jnp.log(lb), jnp.log1p(-lb)

    def features(p):
        B, T = p.shape[:2]
        hd = lambda t: t.reshape(B, T, HGRN_HEADS, HEAD_DIM)
        q, zf, zb, i, g = split_cols(p.astype(f32), HGRN_COLS)
        gates = []
        for z in (zf, zb):
            log_f = jnp.logaddexp(log_lb, log_ub + jax.nn.log_sigmoid(z))
            k = (1.0 - lb) * jax.nn.sigmoid(-z)
            gates.append((hd(log_f), hd(k)))
        return hd(q), hd(i), gates, g

    def readout(o, g, dtype):
        B, T = o.shape[:2]
        return (rms_norm(o, norm_w).reshape(B, T, BRANCH_WIDTH) * jax.nn.silu(g)).astype(dtype)

    qc, ic, gates_c, gc = features(p_c)
    ql, il, gates_l, gl = features(p_l)
    assert p_l.shape[0] == 1
    C = p_c.shape[1]
    W = BRANCH_WIDTH

    nat = lambda tc, tl: jnp.concatenate([tc[0], tl[0]], axis=0).reshape(-1, W)
    q_all, i_all = nat(qc, ql), nat(ic, il)
    o = sum(hgrn_scan_pallas(q_all, nat(gates_c[d][0], gates_l[d][0]), nat(gates_c[d][1], gates_l[d][1]), i_all,
                             bool(d), C // HGRN_BLOCK) for d in range(2))
    o = o.reshape(-1, HGRN_HEADS, HEAD_DIM)
    lat = readout(o[C:][None], gl, p_l.dtype)
    ctx_out = readout(o[:C][None], gc, p_c.dtype) if need_ctx else None
    return ctx_out, lat


def gated_merge(gate_cols, branches, b_merge, w_branch, w_out):
    B, T = gate_cols.shape[:2]
    gates = jax.nn.sigmoid(gate_cols + b_merge).reshape(B, T, N_BRANCH, D_MODEL)
    br = jnp.stack(branches, axis=2)
    proj = jnp.einsum('btnc,ncd->btnd', br, w_branch)
    return jnp.sum(gates * proj, axis=2) @ w_out


MOE_ROWS = 256


def _moe_expert_kernel(be_ref, nb_ref, x_ref, wgu_ref, wd_ref, bgu_ref, bd_ref, sel_ref, y_ref):
    f32 = jnp.float32

    @pl.when(pl.program_id(0) < nb_ref[0])
    def _():
        gu = jnp.dot(x_ref[...], wgu_ref[0], preferred_element_type=f32) + bgu_ref[0]
        gate = jnp.minimum(gu, SWIGLU_LIMIT)
        up = jnp.clip(pltpu.roll(gu, gu.shape[1] - 1, axis=1), -SWIGLU_LIMIT, SWIGLU_LIMIT)
        glu = gate * jax.nn.sigmoid(SWIGLU_ALPHA * gate)
        act2 = ((up + 1.0) * glu).astype(jnp.bfloat16)
        act = jnp.dot(act2, sel_ref[...], preferred_element_type=f32).astype(jnp.bfloat16)
        y_ref[...] = jnp.dot(act, wd_ref[0], preferred_element_type=f32) + bd_ref[0]

    @pl.when(pl.program_id(0) >= nb_ref[0])
    def _():
        y_ref[...] = jnp.zeros_like(y_ref)


def moe_experts_pallas(xb, block_e, n_used, wgu, wd, bgu, bd):
    cap, D = xb.shape
    F2 = wgu.shape[-1]
    F = F2 // 2
    R = MOE_ROWS
    n_blocks = cap // R
    sel = (jnp.arange(F2, dtype=jnp.int32)[:, None] == 2 * jnp.arange(F, dtype=jnp.int32)[None, :]).astype(jnp.bfloat16)
    grid_spec = pltpu.PrefetchScalarGridSpec(
        num_scalar_prefetch=2,
        grid=(n_blocks,),
        in_specs=[pl.BlockSpec((R, D), lambda i, be, nb: (i, 0)),
                  pl.BlockSpec((1, D, F2), lambda i, be, nb: (be[i], 0, 0)),
                  pl.BlockSpec((1, F, D), lambda i, be, nb: (be[i], 0, 0)),
                  pl.BlockSpec((1, 1, F2), lambda i, be, nb: (be[i], 0, 0)),
                  pl.BlockSpec((1, 1, D), lambda i, be, nb: (be[i], 0, 0)),
                  pl.BlockSpec((F2, F), lambda i, be, nb: (0, 0))],
        out_specs=pl.BlockSpec((R, D), lambda i, be, nb: (i, 0)),
    )
    return pl.pallas_call(
        _moe_expert_kernel,
        grid_spec=grid_spec,
        out_shape=jax.ShapeDtypeStruct((cap, D), jnp.float32),
        compiler_params=pltpu.CompilerParams(dimension_semantics=("arbitrary",),
                                             vmem_limit_bytes=MOE_VMEM_LIMIT_BYTES),
    )(block_e, n_used, xb, wgu, wd, bgu, bd, sel)


def moe_ffn(h, router_w, router_b, gu_w, gu_b, dn_w, dn_b):
    N, D = h.shape
    R = MOE_ROWS
    logits = (jnp.dot(h, router_w, precision=lax.Precision.HIGHEST) + router_b).astype(jnp.float32)
    top_l, top_e = lax.top_k(logits, TOP_K)
    weights = jax.nn.softmax(top_l, axis=-1)
    n_assign = N * TOP_K
    flat_e = top_e.reshape(-1)
    order = jnp.argsort(flat_e)
    e_sorted = flat_e[order]
    tok_sorted = (jnp.arange(n_assign, dtype=jnp.int32) // TOP_K)[order]
    counts = jnp.bincount(flat_e, length=N_EXPERTS)
    padded = (counts + R - 1) // R * R
    start_sorted = jnp.cumsum(counts) - counts
    pad_end = jnp.cumsum(padded)
    start_pad = pad_end - padded
    dest = (start_pad[e_sorted] + jnp.arange(n_assign, dtype=jnp.int32) - start_sorted[e_sorted]).astype(jnp.int32)
    n_blocks = -(-(n_assign + N_EXPERTS * (R - 1)) // R)
    cap = n_blocks * R
    slot_tok = jnp.full((cap,), N, dtype=jnp.int32).at[dest].set(tok_sorted)
    slot_of_assign = jnp.zeros((n_assign,), jnp.int32).at[order].set(dest)
    block_e = jnp.minimum(jnp.searchsorted(pad_end, jnp.arange(n_blocks, dtype=jnp.int32) * R, side='right'),
                          N_EXPERTS - 1).astype(jnp.int32)
    n_used = (pad_end[-1] // R).astype(jnp.int32).reshape(1)
    h_pad = jnp.concatenate([h.astype(jnp.bfloat16), jnp.zeros((1, D), jnp.bfloat16)], axis=0)
    xb = h_pad[slot_tok]
    bf16 = jnp.bfloat16
    yb = moe_experts_pallas(xb, block_e, n_used, gu_w.astype(bf16), dn_w.astype(bf16),
                            gu_b[:, None, :], dn_b[:, None, :])
    picked = yb[slot_of_assign].reshape(N, TOP_K, D)
    return jnp.sum(picked * weights.astype(h.dtype)[:, :, None], axis=1)


def _mm_kernel(a_ref, b_ref, o_ref):
    o_ref[...] = jnp.dot(a_ref[...], b_ref[...], preferred_element_type=jnp.float32)


def _pick(n, cands):
    for c in cands:
        if n % c == 0:
            return c
    return n


def pallas_matmul(a, b):
    M, K = a.shape
    N = b.shape[1]
    tm = _pick(M, (1056, 1024, 768, 512, 256, 128))
    tn = _pick(N, (1152, 1024, 640, 512, 256, 128))
    return pl.pallas_call(
        _mm_kernel,
        grid=(M // tm, N // tn),
        in_specs=[pl.BlockSpec((tm, K), lambda i, j: (i, 0)),
                  pl.BlockSpec((K, tn), lambda i, j: (0, j))],
        out_specs=pl.BlockSpec((tm, tn), lambda i, j: (i, j)),
        out_shape=jax.ShapeDtypeStruct((M, N), jnp.float32),
        compiler_params=pltpu.CompilerParams(dimension_semantics=("parallel", "parallel"),
                                             vmem_limit_bytes=VMEM_LIMIT_BYTES),
    )(a, b)


def kernel(x, c, ctx, c_ctx, ada_w, ada_b, norm_mix, norm_ffn, w_in, b_merge, qk_norm_q, qk_norm_k, att_sink,
           ssm_lam_re, ssm_lam_im, ssm_log_dt, ssm_b_re, ssm_b_im, ssm_c_re, ssm_c_im, ssm_d, ssm_glu_w, ssm_glu_b,
           rwkv_mu_prev, rwkv_mu_next, rwkv_w0, rwkv_w2, rwkv_a0, rwkv_a2, rwkv_g2, rwkv_k_k, rwkv_k_a, rwkv_r_k,
           rwkv_lnx_w, rwkv_lnx_b, hgrn_lb_raw, hgrn_norm, w_branch, w_out, router_w, router_b,
           exp_gu_w, exp_gu_b, exp_down_w, exp_down_b):
    B, S, D = x.shape
    C = ctx.shape[1]
    rows = S // GRID_W
    cos, sin = axial_rope(rows)
    lb_all = jnp.cumsum(jax.nn.softmax(hgrn_lb_raw.astype(jnp.float32), axis=0), axis=0)
    lb_all = lb_all - lb_all[:1]
    h_ctx = ctx
    for l in range(DEPTH):
        need_ctx = l < DEPTH - 1
        m_lat = [t[:, None, :] for t in jnp.split(jax.nn.silu(c) @ ada_w[l] + ada_b[l], 6, axis=-1)]
        m_ctx = jnp.split(jax.nn.silu(c_ctx) @ ada_w[l] + ada_b[l], 6, axis=-1)
        hl = modulate(rms_norm(x, norm_mix[l]), m_lat[0], m_lat[1])
        hc = modulate(rms_norm(h_ctx, norm_mix[l]), m_ctx[0], m_ctx[1])
        h_all = jnp.concatenate([hc[0], hl[0]], axis=0).astype(jnp.bfloat16)
        n_mix = D_IN - N_BRANCH * D_MODEL
        p_mix = pallas_matmul(h_all, w_in[l][:, :n_mix].astype(jnp.bfloat16))
        p_gate = pallas_matmul(h_all, w_in[l][:, n_mix:].astype(jnp.bfloat16))
        mix_cols = IN_COLS[:-1]
        qc_, kc_, vc_, uc, rwc, hgc = split_cols(p_mix[:C][None], mix_cols)
        ql_, kl_, vl_, ul, rwl, hgl = split_cols(p_mix[C:][None], mix_cols)
        gtc, gtl = p_gate[:C][None], p_gate[C:][None]
        att_c, att_l = attention_branch(qc_, kc_, vc_, ql_, kl_, vl_, qk_norm_q[l], qk_norm_k[l], att_sink[l],
                                        cos, sin, need_ctx)
        ssm_c, ssm_l = s5_branch(uc, ul, ssm_lam_re[l], ssm_lam_im[l], ssm_log_dt[l], ssm_b_re[l], ssm_b_im[l],
                                 ssm_c_re[l], ssm_c_im[l], ssm_d[l], ssm_glu_w[l], ssm_glu_b[l], need_ctx)
        rw_c, rw_l = rwkv_branch(rwc, rwl, rwkv_mu_prev[l], rwkv_mu_next[l], rwkv_w0[l], rwkv_w2[l], rwkv_a0[l],
                                 rwkv_a2[l], rwkv_g2[l], rwkv_k_k[l], rwkv_k_a[l], rwkv_r_k[l], rwkv_lnx_w[l],
                                 rwkv_lnx_b[l], need_ctx)
        hg_c, hg_l = hgrn_branch(hgc, hgl, lb_all[l], hgrn_norm[l], need_ctx)
        x = x + m_lat[2] * gated_merge(gtl, (att_l, ssm_l, rw_l, hg_l), b_merge[l], w_branch[l], w_out[l])
        hl2 = modulate(rms_norm(x, norm_ffn[l]), m_lat[3], m_lat[4])
        moe_args = (router_w[l], router_b[l], exp_gu_w[l], exp_gu_b[l], exp_down_w[l], exp_down_b[l])
        if need_ctx:
            h_ctx = h_ctx + m_ctx[2] * gated_merge(gtc, (att_c, ssm_c, rw_c, hg_c), b_merge[l], w_branch[l], w_out[l])
            hc2 = modulate(rms_norm(h_ctx, norm_ffn[l]), m_ctx[3], m_ctx[4])
            tokens = jnp.concatenate([hc2.reshape(B * C, D), hl2.reshape(B * S, D)], axis=0)
            y = moe_ffn(tokens, *moe_args)
            h_ctx = h_ctx + m_ctx[5] * y[:B * C].reshape(B, C, D)
            x = x + m_lat[5] * y[B * C:].reshape(B, S, D)
        else:
            y = moe_ffn(hl2.reshape(B * S, D), *moe_args)
            x = x + m_lat[5] * y.reshape(B, S, D)
    return x
```

```python
import functools
import math
import jax
import jax.numpy as jnp
from jax import lax
import numpy as np
from jax.experimental import pallas as pl
from jax.experimental.pallas import tpu as pltpu

D_MODEL = 2048
BATCH = 1
SEQ = 8192
DEPTH = 2

GRID_W = 64
CTX_LEN = 256
HEAD_DIM = 64
N_BRANCH = 4
BRANCH_WIDTH = D_MODEL // 4

ATT_HEADS = BRANCH_WIDTH // HEAD_DIM
ATT_KV_HEADS = ATT_HEADS // 4
ATT_GROUP = ATT_HEADS // ATT_KV_HEADS
WINDOW = 128
ATT_BLOCK = 128
ROPE_BASE = 10000.0

SSM_GROUP = 16
SSM_GROUPS = BRANCH_WIDTH // SSM_GROUP
SSM_STATE = 64

RWKV_HEADS = BRANCH_WIDTH // HEAD_DIM
DECAY_LORA = 64
ICLR_LORA = 64
GATE_LORA = 128
RWKV_GN_EPS = 64e-5

HGRN_HEADS = BRANCH_WIDTH // HEAD_DIM
HGRN_CHUNK = 64

N_EXPERTS = 32
TOP_K = 4
D_EXPERT = D_MODEL // 2
SWIGLU_LIMIT = 7.0
SWIGLU_ALPHA = 1.702
MOE_BLOCK = 128

NORM_EPS = 1e-6

ATT_COLS = (ATT_HEADS * HEAD_DIM, ATT_KV_HEADS * HEAD_DIM, ATT_KV_HEADS * HEAD_DIM)
RWKV_COLS = (BRANCH_WIDTH, BRANCH_WIDTH, BRANCH_WIDTH, DECAY_LORA, DECAY_LORA, ICLR_LORA, ICLR_LORA, GATE_LORA)
HGRN_COLS = (BRANCH_WIDTH,) * 5
IN_COLS = ATT_COLS + (BRANCH_WIDTH, sum(RWKV_COLS), sum(HGRN_COLS), N_BRANCH * D_MODEL)
D_IN = sum(IN_COLS)

VMEM_LIMIT_BYTES = 48 * 1024 * 1024
MOE_VMEM_LIMIT_BYTES = 56 * 1024 * 1024


def split_cols(p, sizes):
    idx = [sum(sizes[:i + 1]) for i in range(len(sizes) - 1)]
    return jnp.split(p, idx, axis=-1)


def rms_norm(x, w):
    xf = x.astype(jnp.float32)
    y = xf * lax.rsqrt(jnp.mean(xf * xf, axis=-1, keepdims=True) + NORM_EPS)
    return (y * w).astype(x.dtype)


def modulate(h, shift, scale):
    return h * (1.0 + scale) + shift


def axial_rope(rows):
    n_freq = HEAD_DIM // 4
    inv = ROPE_BASE ** (-jnp.arange(n_freq, dtype=jnp.float32) / n_freq)
    row = jnp.repeat(jnp.arange(rows, dtype=jnp.float32), GRID_W)
    col = jnp.tile(jnp.arange(GRID_W, dtype=jnp.float32), rows)
    ang = jnp.concatenate([row[:, None] * inv, col[:, None] * inv], axis=-1)
    return jnp.cos(ang), jnp.sin(ang)


def apply_rope(x, cos, sin):
    xf = x.astype(jnp.float32)
    half = HEAD_DIM // 2
    x1, x2 = xf[..., :half], xf[..., half:]
    cs, sn = cos[None, :, None, :], sin[None, :, None, :]
    return jnp.concatenate([x1 * cs - x2 * sn, x1 * sn + x2 * cs], axis=-1).astype(x.dtype)


def softmax_with_sink(logits, sink):
    m = jnp.maximum(jnp.max(logits, axis=-1, keepdims=True), sink)
    e = jnp.exp(logits - m)
    return e / (jnp.sum(e, axis=-1, keepdims=True) + jnp.exp(sink - m))


def attention_branch(q_c, k_c, v_c, q_l, k_l, v_l, qn_w, kn_w, sink, cos, sin, need_ctx):
    B, S = q_l.shape[:2]
    C = q_c.shape[1]
    heads = lambda t, h: t.reshape(t.shape[0], t.shape[1], h, HEAD_DIM)
    kc = rms_norm(heads(k_c, ATT_KV_HEADS), kn_w)
    vc = heads(v_c, ATT_KV_HEADS)
    ql = apply_rope(rms_norm(heads(q_l, ATT_HEADS), qn_w), cos, sin)
    kl = apply_rope(rms_norm(heads(k_l, ATT_KV_HEADS), kn_w), cos, sin)
    vl = heads(v_l, ATT_KV_HEADS)
    scale = HEAD_DIM ** -0.5
    sink_f = sink.astype(jnp.float32).reshape(ATT_KV_HEADS, ATT_GROUP)
    nb = S // ATT_BLOCK
    qb = ql.reshape(B, nb, ATT_BLOCK, ATT_KV_HEADS, ATT_GROUP, HEAD_DIM)

    def band(t):
        tp = jnp.pad(t, ((0, 0), (ATT_BLOCK, ATT_BLOCK), (0, 0), (0, 0)))
        tp = tp.reshape(B, nb + 2, ATT_BLOCK, ATT_KV_HEADS, HEAD_DIM)
        return jnp.concatenate([tp[:, :-2], tp[:, 1:-1], tp[:, 2:]], axis=2)

    kb, vb = band(kl), band(vl)
    s_win = jnp.einsum('bnqhgd,bnkhd->bnhgqk', qb, kb, preferred_element_type=jnp.float32) * scale
    q_pos = jnp.arange(nb)[:, None] * ATT_BLOCK + jnp.arange(ATT_BLOCK)[None, :]
    k_pos = jnp.arange(nb)[:, None] * ATT_BLOCK + jnp.arange(3 * ATT_BLOCK)[None, :] - ATT_BLOCK
    kp = k_pos[:, None, :]
    valid = (jnp.abs(kp - q_pos[:, :, None]) <= WINDOW) & (kp >= 0) & (kp < S)
    s_win = jnp.where(valid[None, :, None, None], s_win, -jnp.inf)
    s_ctx = jnp.einsum('bnqhgd,bkhd->bnhgqk', qb, kc, preferred_element_type=jnp.float32) * scale
    p = softmax_with_sink(jnp.concatenate([s_win, s_ctx], axis=-1), sink_f[None, None, :, :, None, None])
    p = p.astype(vl.dtype)
    o = (jnp.einsum('bnhgqk,bnkhd->bnqhgd', p[..., :3 * ATT_BLOCK], vb)
         + jnp.einsum('bnhgqk,bkhd->bnqhgd', p[..., 3 * ATT_BLOCK:], vc))
    lat = o.reshape(B, S, ATT_HEADS * HEAD_DIM)
    ctx_out = None
    if need_ctx:
        qc = rms_norm(heads(q_c, ATT_HEADS), qn_w).reshape(B, C, ATT_KV_HEADS, ATT_GROUP, HEAD_DIM)
        s_cc = jnp.einsum('bqhgd,bkhd->bhgqk', qc, kc, preferred_element_type=jnp.float32) * scale
        pc = softmax_with_sink(s_cc, sink_f[None, :, :, None, None]).astype(vc.dtype)
        ctx_out = jnp.einsum('bhgqk,bkhd->bqhgd', pc, vc).reshape(B, C, ATT_HEADS * HEAD_DIM)
    return ctx_out, lat


S5_BLOCK = 128


def _s5_scan_kernel(reverse, u_ref, bre_ref, bim_ref, pwr_ref, pwi_ref, cre_ref, cim_ref, y_ref, car_ref):
    TB = S5_BLOCK
    f32 = jnp.float32

    @pl.when(pl.program_id(0) == 0)
    def _():
        car_ref[...] = jnp.zeros_like(car_ref)

    u = u_ref[...].astype(jnp.bfloat16)
    xr = jnp.dot(u, bre_ref[...], preferred_element_type=f32)
    xi = jnp.dot(u, bim_ref[...], preferred_element_type=f32)
    rows = lax.broadcasted_iota(jnp.int32, (TB, 1), 0)
    s = 1
    while s < TB:
        p = TB - s if reverse else s - 1
        ar = pwr_ref[p:p + 1, :]
        ai = pwi_ref[p:p + 1, :]
        keep = rows < TB - s if reverse else rows >= s
        shift = TB - s if reverse else s
        sr = jnp.where(keep, pltpu.roll(xr, shift, axis=0), 0.0)
        si = jnp.where(keep, pltpu.roll(xi, shift, axis=0), 0.0)
        xr, xi = xr + (ar * sr - ai * si), xi + (ar * si + ai * sr)
        s *= 2
    cr = car_ref[0:1, :]
    ci = car_ref[1:2, :]
    pr = pwr_ref[...]
    pi_ = pwi_ref[...]
    xr = xr + (pr * cr - pi_ * ci)
    xi = xi + (pr * ci + pi_ * cr)
    last = 0 if reverse else TB - 1
    car_ref[0:1, :] = xr[last:last + 1, :]
    car_ref[1:2, :] = xi[last:last + 1, :]
    y_ref[...] = (jnp.dot(xr.astype(jnp.bfloat16), cre_ref[...], preferred_element_type=f32)
                  - jnp.dot(xi.astype(jnp.bfloat16), cim_ref[...], preferred_element_type=f32))


def s5_scan_pallas(u, bt_re, bt_im, pw_re, pw_im, c_re, c_im, reverse, n_ctx_blocks):
    T, W = u.shape
    N = bt_re.shape[-1]
    TB = S5_BLOCK
    nb = T // TB
    blk = (lambda i: (_backward_block(i, n_ctx_blocks, nb), 0)) if reverse else (lambda i: (i, 0))
    const = lambda i: (0, 0)
    return pl.pallas_call(
        functools.partial(_s5_scan_kernel, reverse),
        grid=(nb,),
        in_specs=[pl.BlockSpec((TB, W), blk),
                  pl.BlockSpec((W, N), const),
                  pl.BlockSpec((W, N), const),
                  pl.BlockSpec((TB, N), const),
                  pl.BlockSpec((TB, N), const),
                  pl.BlockSpec((N, W), const),
                  pl.BlockSpec((N, W), const)],
        out_specs=pl.BlockSpec((TB, W), blk),
        out_shape=jax.ShapeDtypeStruct((T, W), jnp.float32),
        scratch_shapes=[pltpu.VMEM((2, N), jnp.float32)],
        compiler_params=pltpu.CompilerParams(dimension_semantics=("arbitrary",),
                                             vmem_limit_bytes=VMEM_LIMIT_BYTES),
    )(u, bt_re, bt_im, pw_re, pw_im, c_re, c_im)


def s5_branch(u_c, u_l, lam_re, lam_im, log_dt, b_re, b_im, c_re, c_im, d_skip, glu_w, glu_b, need_ctx):
    f32 = jnp.float32
    assert u_l.shape[0] == 1
    C = u_c.shape[1]
    G, P, GC = SSM_GROUPS, SSM_STATE, SSM_GROUP
    dt = jnp.exp(log_dt.astype(f32))[..., None]
    lr, li = lam_re.astype(f32), lam_im.astype(f32)
    mag = jnp.exp(dt * lr)
    ab_re, ab_im = mag * jnp.cos(dt * li), mag * jnp.sin(dt * li)
    den = lr * lr + li * li
    gm_re = ((ab_re - 1.0) * lr + ab_im * li) / den
    gm_im = (ab_im * lr - (ab_re - 1.0) * li) / den
    eye_g = jnp.eye(G, dtype=f32)
    br = jnp.transpose(b_re.astype(f32), (0, 2, 1))
    bi = jnp.transpose(b_im.astype(f32), (0, 2, 1))
    gr, gi = gm_re[:, :, None, :], gm_im[:, :, None, :]
    blockdiag = lambda m: jnp.einsum('dgcp,gh->dgchp', m, eye_g).reshape(2, G * GC, G * P)
    bt_re = blockdiag(gr * br - gi * bi).astype(jnp.bfloat16)
    bt_im = blockdiag(gr * bi + gi * br).astype(jnp.bfloat16)
    n = jnp.arange(1, S5_BLOCK + 1, dtype=f32)[None, :, None, None]
    theta = dt * li
    theta = theta - (2.0 * math.pi) * jnp.round(theta / (2.0 * math.pi))
    pmag = jnp.exp(n * (dt * lr)[:, None])
    pw_re = (pmag * jnp.cos(n * theta[:, None])).reshape(2, S5_BLOCK, G * P)
    pw_im = (pmag * jnp.sin(n * theta[:, None])).reshape(2, S5_BLOCK, G * P)
    cbd = lambda m: jnp.einsum('gcp,gh->gphc', m.astype(f32), eye_g).reshape(G * P, G * GC).astype(jnp.bfloat16)
    uc, ul = u_c[0].astype(f32), u_l[0].astype(f32)
    u_all = jnp.concatenate([uc, ul], axis=0)
    cr_bd, ci_bd = cbd(c_re), cbd(c_im)
    yp = (s5_scan_pallas(u_all, bt_re[0], bt_im[0], pw_re[0], pw_im[0], cr_bd, ci_bd, False, C // S5_BLOCK)
          + s5_scan_pallas(u_all, bt_re[1], bt_im[1], pw_re[1, ::-1], pw_im[1, ::-1], cr_bd, ci_bd, True,
                           C // S5_BLOCK))

    def readout(y, u):
        y = y + d_skip * u.astype(f32)
        z = jax.nn.gelu(y) @ glu_w + glu_b
        za, zb = jnp.split(z, 2, axis=-1)
        return (za * jax.nn.sigmoid(zb)).astype(u.dtype)

    lat = readout(yp[C:][None], u_l)
    ctx_out = readout(yp[:C][None], u_c) if need_ctx else None
    return ctx_out, lat


def shift_mix(p, mu_prev, mu_next):
    zero = jnp.zeros_like(p[:, :1])
    prev = jnp.concatenate([zero, p[:, :-1]], axis=1)
    nxt = jnp.concatenate([p[:, 1:], zero], axis=1)
    return p + mu_prev * (prev - p) + mu_next * (nxt - p)


RWKV_CHUNK = 64
RWKV_INV_BLOCK = 16

_NN = (((1,), (0,)), ((), ()))
_NT = (((1,), (1,)), ((), ()))
_TN = (((0,), (0,)), ((), ()))


def _split_bf16(x):
    hi = x.astype(jnp.bfloat16)
    lo = (x - hi.astype(jnp.float32)).astype(jnp.bfloat16)
    return hi, lo


def _dot1(a, b, dims=_NN):
    return lax.dot_general(a.astype(jnp.bfloat16), b.astype(jnp.bfloat16), dims,
                           preferred_element_type=jnp.float32)


def _dot3(a, b, dims=_NN):
    ah, al = _split_bf16(a)
    bh, bl = _split_bf16(b)
    d = functools.partial(lax.dot_general, dimension_numbers=dims, preferred_element_type=jnp.float32)
    return d(ah, bh) + (d(ah, bl) + d(al, bh))


def _each(fn, *lists):
    return [fn(*args) for args in zip(*lists)]


def _rwkv_chunk_maps(lw, r, kk, be, kd, v, rev):
    L = RWKV_CHUNK
    f32 = jnp.float32
    row = lax.broadcasted_iota(jnp.int32, (L, L), 0)
    col = lax.broadcasted_iota(jnp.int32, (L, L), 1)
    ahead = jnp.where(rev, col - row, row - col)
    tril_i = ahead >= 0
    tril_s = ahead > 0
    blk = (row // RWKV_INV_BLOCK) == (col // RWKV_INV_BLOCK)
    eye = jnp.where(row == col, 1.0, 0.0).astype(f32)
    tri = jnp.where(tril_i, 1.0, 0.0).astype(jnp.bfloat16)
    dd = functools.partial(jnp.dot, preferred_element_type=f32)

    def cumsum(x):
        h1 = x.astype(jnp.bfloat16)
        r1 = x - h1.astype(f32)
        h2 = r1.astype(jnp.bfloat16)
        h3 = (r1 - h2.astype(f32)).astype(jnp.bfloat16)
        return dd(tri, h1) + (dd(tri, h2) + dd(tri, h3))

    c = _each(cumsum, lw)
    c_end = [jnp.where(rev, x[0:1, :], x[L - 1:L, :]) for x in c]
    e_c = [jnp.exp(x) for x in c]
    e_nc = [jnp.exp(-x) for x in c]
    e_end = [jnp.exp(x) for x in c_end]
    al = _each(lambda a, x, w: a * jnp.exp(x - w), kk, c, lw)
    rt = _each(lambda a, e: a * e, r, e_c)
    bb = _each(lambda a, e: a * e, be, e_nc)
    kb = _each(lambda a, e: a * e, kd, e_nc)
    bh = _each(lambda a, e: a * e, bb, e_end)
    kh = _each(lambda a, e: a * e, kb, e_end)
    lhs = _each(lambda a, b: jnp.concatenate([a, b], axis=0), al, rt)
    g1 = _each(lambda a, b: _dot1(a, b, _NT), lhs, bb)
    g2 = _each(lambda a, b: _dot1(a, b, _NT), lhs, kb)
    n = [jnp.where(tril_s, g[:L], 0.0) for g in g1]
    arb = [jnp.where(tril_i, g[L:], 0.0) for g in g1]
    aak = [jnp.where(tril_s, g[:L], 0.0) for g in g2]
    ark = [jnp.where(tril_i, g[L:], 0.0) for g in g2]
    d1 = [jnp.where(blk, x, 0.0) for x in n]
    o1 = _each(lambda a, b: a - b, n, d1)
    d2 = _each(_dot1, d1, d1)
    d4 = _each(_dot1, d2, d2)
    d8 = _each(_dot1, d4, d4)
    imd = [eye - x for x in d1]
    t1 = _each(lambda a, b: a + _dot1(a, b), imd, d2)
    t2 = _each(lambda a, b: a + _dot1(a, b), t1, d4)
    dinv = _each(lambda a, b: a + _dot1(a, b), t2, d8)
    e1 = _each(_dot1, dinv, o1)
    e2 = _each(_dot1, e1, e1)
    ime = [eye - x for x in e1]
    fm = _each(lambda a, b: a + _dot1(a, b), ime, e2)
    minv = _each(_dot1, fm, dinv)
    at = _each(_dot1, minv, al)
    av = _each(_dot1, aak, v)
    pm = _each(_dot1, minv, av)
    rh = _each(lambda a, b, x: a - _dot1(b, x), rt, arb, at)
    y0 = _each(lambda a, x, b, p: _dot1(a, x) - _dot1(b, p), ark, v, arb, pm)
    gt = _each(lambda e, b, a: eye * e - _dot1(b, a, _TN), e_end, bh, at)
    ht = _each(lambda k, x, b, p: _dot1(k, x, _TN) - _dot1(b, p, _TN), kh, v, bh, pm)
    return gt, ht, rh, y0


def _rwkv_prep_kernel(lw_ref, be_ref, kd_ref, r_ref, kk_ref, v_ref, gt_ref, ht_ref, rh_ref, y0_ref):
    rev = pl.program_id(0) == 1
    hs = range(RWKV_HEADS)
    gt, ht, rh, y0 = _rwkv_chunk_maps([lw_ref[0, h] for h in hs], [r_ref[h] for h in hs], [kk_ref[h] for h in hs],
                                      [be_ref[0, h] for h in hs], [kd_ref[0, h] for h in hs],
                                      [v_ref[h] for h in hs], rev)
    for h in hs:
        gt_ref[0, h, 0] = gt[h]
        ht_ref[0, h, 0] = ht[h]
        rh_ref[0, h, 0] = rh[h]
        y0_ref[0, h, 0] = y0[h]


def _rwkv_scan_kernel(gt_ref, ht_ref, rh_ref, y0_ref, yf_ref, yb_ref, st_ref):
    @pl.when(pl.program_id(0) == 0)
    def _():
        st_ref[...] = jnp.zeros_like(st_ref)

    for d, y_ref in enumerate((yf_ref, yb_ref)):
        for h in range(RWKV_HEADS):
            st = st_ref[d, h]
            y_ref[h] = _dot3(rh_ref[d, h, 0], st) + y0_ref[d, h, 0]
            st_ref[d, h] = _dot3(gt_ref[d, h, 0], st) + ht_ref[d, h, 0]


def _backward_block(i, n_ctx, n_all):
    return jnp.where(i < n_ctx, n_ctx - 1 - i, n_all - 1 - (i - n_ctx))


def rwkv_scan_pallas(lw, be, kd, r, kk, v, n_ctx_chunks):
    _, H, T, N = lw.shape
    L = RWKV_CHUNK
    nc = T // L
    chunk = lambda d, i: jnp.where(d == 0, i, _backward_block(i, n_ctx_chunks, nc))
    dir_spec = pl.BlockSpec((1, H, L, N), lambda d, i: (d, 0, chunk(d, i), 0))
    shared_spec = pl.BlockSpec((H, L, N), lambda d, i: (0, chunk(d, i), 0))
    map_spec = pl.BlockSpec((1, H, 1, N, N), lambda d, i: (d, 0, i, 0, 0))
    map_shape = jax.ShapeDtypeStruct((2, H, nc, N, N), jnp.float32)
    gt, ht, rh, y0 = pl.pallas_call(
        _rwkv_prep_kernel,
        grid=(2, nc),
        in_specs=[dir_spec] * 3 + [shared_spec] * 3,
        out_specs=[map_spec] * 4,
        out_shape=[map_shape] * 4,
        compiler_params=pltpu.CompilerParams(dimension_semantics=("parallel", "parallel"),
                                             vmem_limit_bytes=VMEM_LIMIT_BYTES),
    )(lw, be, kd, r, kk, v)
    scan_spec = pl.BlockSpec((2, H, 1, N, N), lambda i: (0, 0, i, 0, 0))
    y_shape = jax.ShapeDtypeStruct((H, T, N), jnp.float32)
    return pl.pallas_call(
        _rwkv_scan_kernel,
        grid=(nc,),
        in_specs=[scan_spec] * 4,
        out_specs=[pl.BlockSpec((H, L, N), lambda i: (0, i, 0)),
                   pl.BlockSpec((H, L, N), lambda i: (0, _backward_block(i, n_ctx_chunks, nc), 0))],
        out_shape=[y_shape, y_shape],
        scratch_shapes=[pltpu.VMEM((2, H, N, N), jnp.float32)],
        compiler_params=pltpu.CompilerParams(dimension_semantics=("arbitrary",),
                                             vmem_limit_bytes=VMEM_LIMIT_BYTES),
    )(gt, ht, rh, y0)


def rwkv_branch(p_c, p_l, mu_prev, mu_next, w0, w2, a0, a2, g2, k_k, k_a, r_k, lnx_w, lnx_b, need_ctx):
    f32 = jnp.float32
    W = BRANCH_WIDTH
    assert p_l.shape[0] == 1
    C = p_c.shape[1]

    def features(p):
        B, T = p.shape[:2]
        hd = lambda t: t.reshape(B, T, RWKV_HEADS, HEAD_DIM)
        pm = shift_mix(p.astype(f32), mu_prev, mu_next)
        r, k, v, w1f, w1b, a1f, a1b, g1 = split_cols(pm, RWKV_COLS)
        kk = hd(k * k_k)
        kk = kk * lax.rsqrt(jnp.maximum(jnp.sum(kk * kk, axis=-1, keepdims=True), 1e-12))
        dirs = []
        for d, (w1, a1) in enumerate(((w1f, a1f), (w1b, a1b))):
            w = -jax.nn.softplus(-(w0[d] + jnp.tanh(w1) @ w2[d])) - 0.5
            log_decay = -jnp.exp(w)
            a = jax.nn.sigmoid(a0[d] + a1 @ a2[d])
            k_d = k * (1.0 + (a - 1.0) * k_a)
            dirs.append((hd(log_decay), hd(k_d), hd(a)))
        g = jax.nn.sigmoid(g1) @ g2
        return hd(r), hd(v), kk, dirs, g

    def readout(y, r, v, dirs, g, dtype):
        B, T = y.shape[:2]
        mu = jnp.mean(y, axis=-1, keepdims=True)
        var = jnp.mean(jnp.square(y - mu), axis=-1, keepdims=True)
        yn = ((y - mu) * lax.rsqrt(var + RWKV_GN_EPS)).reshape(B, T, W) * lnx_w + lnx_b
        bonus = jnp.sum(r * (dirs[0][1] + dirs[1][1]) * r_k, axis=-1, keepdims=True) * v
        return ((yn + bonus.reshape(B, T, W)) * g).astype(dtype)

    rc, vc, kkc, dirs_c, gc = features(p_c)
    rl, vl, kkl, dirs_l, gl = features(p_l)

    def nat(tc, tl):
        return jnp.concatenate([tc[0], tl[0]], axis=0).transpose(1, 0, 2)

    both = lambda f: jnp.stack([f(0), f(1)])
    lw = both(lambda d: nat(dirs_c[d][0], dirs_l[d][0]))
    kd = both(lambda d: nat(dirs_c[d][1], dirs_l[d][1]))
    be = both(lambda d: nat(kkc * dirs_c[d][2], kkl * dirs_l[d][2]))
    y_f, y_b = rwkv_scan_pallas(lw, be, kd, nat(rc, rl), nat(kkc, kkl), nat(vc, vl), C // RWKV_CHUNK)
    y = (y_f + y_b).transpose(1, 0, 2)
    lat = readout(y[C:][None], rl, vl, dirs_l, gl, p_l.dtype)
    ctx_out = readout(y[:C][None], rc, vc, dirs_c, gc, p_c.dtype) if need_ctx else None
    return ctx_out, lat


HGRN_SUB = 16
HGRN_BLOCK = 128


def _hgrn_kernel(reverse, q_ref, lf_ref, k_ref, v_ref, seg_ref, o_ref, st_ref):
    TB, L = HGRN_BLOCK, HGRN_SUB
    f32 = jnp.float32
    bf16 = jnp.bfloat16

    @pl.when(pl.program_id(0) == 0)
    def _():
        st_ref[...] = jnp.zeros_like(st_ref)

    q = q_ref[...]
    k = k_ref[...]
    v = v_ref[...]
    seg = seg_ref[...]
    pos = lax.broadcasted_iota(jnp.int32, (TB, 1), 0) % L
    if reverse:
        pos = L - 1 - pos
    back = lambda x, j: pltpu.roll(x, TB - j if reverse else j, axis=0)
    b = lf_ref[...]
    s = 1
    while s < L:
        b = b + jnp.where(pos >= s, back(b, s), 0.0)
        s *= 2
    acc = jnp.zeros_like(q)
    for j in range(L):
        if j == 0:
            z = q * k
            vj = v
        else:
            keep = pos >= j
            ex = jnp.exp(jnp.where(keep, b - back(b, j), 0.0))
            z = jnp.where(keep, q * back(k, j) * ex, 0.0)
            vj = back(v, j)
        a = jnp.dot(z.astype(bf16), seg, preferred_element_type=f32)
        acc = acc + a * vj
    o_ref[...] = acc
    eb = jnp.exp(b)
    qs = (q * eb).astype(bf16)
    same_head = seg > 0
    n_sub = TB // L
    for step in range(n_sub):
        i = n_sub - 1 - step if reverse else step
        rows = slice(i * L, (i + 1) * L)
        last = i * L if reverse else (i + 1) * L - 1
        b_end = b[last:last + 1, :]
        st = st_ref[...]
        o_ref[rows, :] += lax.dot_general(qs[rows], st.astype(bf16), _NT, preferred_element_type=f32)
        kt = (k[rows] * jnp.exp(b_end - b[rows])).astype(bf16)
        upd = lax.dot_general(v[rows].astype(bf16), kt, _TN, preferred_element_type=f32)
        st_ref[...] = st * jnp.exp(b_end) + jnp.where(same_head, upd, 0.0)


def hgrn_scan_pallas(q, lf, k, v, reverse, n_ctx_blocks):
    T, W = q.shape
    TB = HGRN_BLOCK
    nb = T // TB
    head = jnp.arange(W, dtype=jnp.int32) // HEAD_DIM
    seg = (head[:, None] == head[None, :]).astype(jnp.bfloat16)
    blk = (lambda i: (_backward_block(i, n_ctx_blocks, nb), 0)) if reverse else (lambda i: (i, 0))
    spec = pl.BlockSpec((TB, W), blk)
    return pl.pallas_call(
        functools.partial(_hgrn_kernel, reverse),
        grid=(nb,),
        in_specs=[spec, spec, spec, spec, pl.BlockSpec((W, W), lambda i: (0, 0))],
        out_specs=spec,
        out_shape=jax.ShapeDtypeStruct((T, W), jnp.float32),
        scratch_shapes=[pltpu.VMEM((W, W), jnp.float32)],
        compiler_params=pltpu.CompilerParams(dimension_semantics=("arbitrary",),
                                             vmem_limit_bytes=VMEM_LIMIT_BYTES),
    )(q, lf, k, v, seg)


def hgrn_branch(p_c, p_l, lb, norm_w, need_ctx):
    f32 = jnp.float32
    log_lb, log_ub = jnp.log(lb), jnp.log1p(-lb)

    def features(p):
        B, T = p.shape[:2]
        hd = lambda t: t.reshape(B, T, HGRN_HEADS, HEAD_DIM)
        q, zf, zb, i, g = split_cols(p.astype(f32), HGRN_COLS)
        gates = []
        for z in (zf, zb):
            log_f = jnp.logaddexp(log_lb, log_ub + jax.nn.log_sigmoid(z))
            k = (1.0 - lb) * jax.nn.sigmoid(-z)
            gates.append((hd(log_f), hd(k)))
        return hd(q), hd(i), gates, g

    def readout(o, g, dtype):
        B, T = o.shape[:2]
        return (rms_norm(o, norm_w).reshape(B, T, BRANCH_WIDTH) * jax.nn.silu(g)).astype(dtype)

    qc, ic, gates_c, gc = features(p_c)
    ql, il, gates_l, gl = features(p_l)
    assert p_l.shape[0] == 1
    C = p_c.shape[1]
    W = BRANCH_WIDTH

    nat = lambda tc, tl: jnp.concatenate([tc[0], tl[0]], axis=0).reshape(-1, W)
    q_all, i_all = nat(qc, ql), nat(ic, il)
    o = sum(hgrn_scan_pallas(q_all, nat(gates_c[d][0], gates_l[d][0]), nat(gates_c[d][1], gates_l[d][1]), i_all,
                             bool(d), C // HGRN_BLOCK) for d in range(2))
    o = o.reshape(-1, HGRN_HEADS, HEAD_DIM)
    lat = readout(o[C:][None], gl, p_l.dtype)
    ctx_out = readout(o[:C][None], gc, p_c.dtype) if need_ctx else None
    return ctx_out, lat


MERGE_ROWS = 256


def _merge_kernel(x_ref, b0_ref, b1_ref, b2_ref, b3_ref, g_ref, bm_ref, wb_ref, wo_ref, m_ref, o_ref):
    f32 = jnp.float32
    D = x_ref.shape[1]
    acc = jnp.zeros(x_ref.shape, f32)
    for n, b_ref in enumerate((b0_ref, b1_ref, b2_ref, b3_ref)):
        proj = jnp.dot(b_ref[...].astype(jnp.bfloat16), wb_ref[n], preferred_element_type=f32)
        gate = jax.nn.sigmoid(g_ref[:, n * D:(n + 1) * D].astype(f32) + bm_ref[:, n * D:(n + 1) * D])
        acc = acc + gate * proj
    out = jnp.dot(acc.astype(jnp.bfloat16), wo_ref[...], preferred_element_type=f32)
    o_ref[...] = x_ref[...] + m_ref[0] * out


def gated_merge_pallas(x, branches, gate_cols, b_merge, w_branch, w_out, m_gate, n_first, gate_row0=0):
    M, D = x.shape
    W = branches[0].shape[1]
    R = MERGE_ROWS
    once = pl.Buffered(1)
    row = lambda i: (i, 0)
    const2 = lambda i: (0, 0)
    return pl.pallas_call(
        _merge_kernel,
        grid=(M // R,),
        in_specs=[pl.BlockSpec((R, D), row)] + [pl.BlockSpec((R, W), row)] * N_BRANCH + [
            pl.BlockSpec((R, N_BRANCH * D), lambda i: (i + gate_row0 // R, 0)),
            pl.BlockSpec((1, N_BRANCH * D), const2, pipeline_mode=once),
            pl.BlockSpec((N_BRANCH, W, D), lambda i: (0, 0, 0), pipeline_mode=once),
            pl.BlockSpec((D, D), const2, pipeline_mode=once),
            pl.BlockSpec((1, 1, D), lambda i: (jnp.where(i < n_first // R, 0, 1), 0, 0))],
        out_specs=pl.BlockSpec((R, D), row),
        out_shape=jax.ShapeDtypeStruct((M, D), jnp.float32),
        compiler_params=pltpu.CompilerParams(dimension_semantics=("parallel",),
                                             vmem_limit_bytes=VMEM_LIMIT_BYTES),
    )(x, *branches, gate_cols, b_merge[None, :], w_branch.astype(jnp.bfloat16), w_out.astype(jnp.bfloat16),
      m_gate[:, None, :])


MOE_ROWS = 256


def _moe_expert_kernel(be_ref, nb_ref, x_ref, wgu_ref, wd_ref, bgu_ref, bd_ref, sel_ref, y_ref):
    f32 = jnp.float32

    @pl.when(pl.program_id(0) < nb_ref[0])
    def _():
        gu = jnp.dot(x_ref[...], wgu_ref[0], preferred_element_type=f32) + bgu_ref[0]
        gate = jnp.minimum(gu, SWIGLU_LIMIT)
        up = jnp.clip(pltpu.roll(gu, gu.shape[1] - 1, axis=1), -SWIGLU_LIMIT, SWIGLU_LIMIT)
        glu = gate * jax.nn.sigmoid(SWIGLU_ALPHA * gate)
        act2 = ((up + 1.0) * glu).astype(jnp.bfloat16)
        act = jnp.dot(act2, sel_ref[...], preferred_element_type=f32).astype(jnp.bfloat16)
        y_ref[...] = (jnp.dot(act, wd_ref[0], preferred_element_type=f32) + bd_ref[0]).astype(y_ref.dtype)

    @pl.when(pl.program_id(0) >= nb_ref[0])
    def _():
        y_ref[...] = jnp.zeros_like(y_ref)


def moe_experts_pallas(xb, block_e, n_used, wgu, wd, bgu, bd):
    cap, D = xb.shape
    F2 = wgu.shape[-1]
    F = F2 // 2
    R = MOE_ROWS
    n_blocks = cap // R
    sel = (jnp.arange(F2, dtype=jnp.int32)[:, None] == 2 * jnp.arange(F, dtype=jnp.int32)[None, :]).astype(jnp.bfloat16)
    grid_spec = pltpu.PrefetchScalarGridSpec(
        num_scalar_prefetch=2,
        grid=(n_blocks,),
        in_specs=[pl.BlockSpec((R, D), lambda i, be, nb: (i, 0)),
                  pl.BlockSpec((1, D, F2), lambda i, be, nb: (be[i], 0, 0)),
                  pl.BlockSpec((1, F, D), lambda i, be, nb: (be[i], 0, 0)),
                  pl.BlockSpec((1, 1, F2), lambda i, be, nb: (be[i], 0, 0)),
                  pl.BlockSpec((1, 1, D), lambda i, be, nb: (be[i], 0, 0)),
                  pl.BlockSpec((F2, F), lambda i, be, nb: (0, 0))],
        out_specs=pl.BlockSpec((R, D), lambda i, be, nb: (i, 0)),
    )
    return pl.pallas_call(
        _moe_expert_kernel,
        grid_spec=grid_spec,
        out_shape=jax.ShapeDtypeStruct((cap, D), jnp.bfloat16),
        compiler_params=pltpu.CompilerParams(dimension_semantics=("arbitrary",),
                                             vmem_limit_bytes=MOE_VMEM_LIMIT_BYTES),
    )(block_e, n_used, xb, wgu, wd, bgu, bd, sel)


def moe_ffn(h, router_w, router_b, gu_w, gu_b, dn_w, dn_b):
    N, D = h.shape
    R = MOE_ROWS
    logits = (jnp.dot(h, router_w, precision=lax.Precision.HIGHEST) + router_b).astype(jnp.float32)
    top_l, top_e = lax.top_k(logits, TOP_K)
    weights = jax.nn.softmax(top_l, axis=-1)
    n_assign = N * TOP_K
    flat_e = top_e.reshape(-1)
    order = jnp.argsort(flat_e)
    e_sorted = flat_e[order]
    tok_sorted = (jnp.arange(n_assign, dtype=jnp.int32) // TOP_K)[order]
    counts = jnp.bincount(flat_e, length=N_EXPERTS)
    padded = (counts + R - 1) // R * R
    start_sorted = jnp.cumsum(counts) - counts
    pad_end = jnp.cumsum(padded)
    start_pad = pad_end - padded
    dest = (start_pad[e_sorted] + jnp.arange(n_assign, dtype=jnp.int32) - start_sorted[e_sorted]).astype(jnp.int32)
    n_blocks = -(-(n_assign + N_EXPERTS * (R - 1)) // R)
    cap = n_blocks * R
    slot_tok = jnp.full((cap,), N, dtype=jnp.int32).at[dest].set(tok_sorted)
    slot_of_assign = jnp.zeros((n_assign,), jnp.int32).at[order].set(dest)
    block_e = jnp.minimum(jnp.searchsorted(pad_end, jnp.arange(n_blocks, dtype=jnp.int32) * R, side='right'),
                          N_EXPERTS - 1).astype(jnp.int32)
    n_used = (pad_end[-1] // R).astype(jnp.int32).reshape(1)
    h_pad = jnp.concatenate([h.astype(jnp.bfloat16), jnp.zeros((1, D), jnp.bfloat16)], axis=0)
    xb = h_pad[slot_tok]
    bf16 = jnp.bfloat16
    yb = moe_experts_pallas(xb, block_e, n_used, gu_w.astype(bf16), dn_w.astype(bf16),
                            gu_b[:, None, :], dn_b[:, None, :])
    picked = yb[slot_of_assign].reshape(N, TOP_K, D)
    return jnp.sum(picked.astype(h.dtype) * weights.astype(h.dtype)[:, :, None], axis=1)


def _mm_kernel(a_ref, b_ref, o_ref):
    o_ref[...] = jnp.dot(a_ref[...], b_ref[...], preferred_element_type=jnp.float32).astype(o_ref.dtype)


def _pick(n, cands):
    for c in cands:
        if n % c == 0:
            return c
    return n


def pallas_matmul(a, b, out_dtype=jnp.float32):
    M, K = a.shape
    N = b.shape[1]
    tm = _pick(M, (1056, 1024, 768, 512, 256, 128))
    tn = _pick(N, (1152, 1024, 640, 512, 256, 128))
    return pl.pallas_call(
        _mm_kernel,
        grid=(M // tm, N // tn),
        in_specs=[pl.BlockSpec((tm, K), lambda i, j: (i, 0)),
                  pl.BlockSpec((K, tn), lambda i, j: (0, j))],
        out_specs=pl.BlockSpec((tm, tn), lambda i, j: (i, j)),
        out_shape=jax.ShapeDtypeStruct((M, N), out_dtype),
        compiler_params=pltpu.CompilerParams(dimension_semantics=("parallel", "parallel"),
                                             vmem_limit_bytes=VMEM_LIMIT_BYTES),
    )(a, b)


def kernel(x, c, ctx, c_ctx, ada_w, ada_b, norm_mix, norm_ffn, w_in, b_merge, qk_norm_q, qk_norm_k, att_sink,
           ssm_lam_re, ssm_lam_im, ssm_log_dt, ssm_b_re, ssm_b_im, ssm_c_re, ssm_c_im, ssm_d, ssm_glu_w, ssm_glu_b,
           rwkv_mu_prev, rwkv_mu_next, rwkv_w0, rwkv_w2, rwkv_a0, rwkv_a2, rwkv_g2, rwkv_k_k, rwkv_k_a, rwkv_r_k,
           rwkv_lnx_w, rwkv_lnx_b, hgrn_lb_raw, hgrn_norm, w_branch, w_out, router_w, router_b,
           exp_gu_w, exp_gu_b, exp_down_w, exp_down_b):
    B, S, D = x.shape
    C = ctx.shape[1]
    rows = S // GRID_W
    cos, sin = axial_rope(rows)
    lb_all = jnp.cumsum(jax.nn.softmax(hgrn_lb_raw.astype(jnp.float32), axis=0), axis=0)
    lb_all = lb_all - lb_all[:1]
    h_ctx = ctx
    for l in range(DEPTH):
        need_ctx = l < DEPTH - 1
        m_lat = [t[:, None, :] for t in jnp.split(jax.nn.silu(c) @ ada_w[l] + ada_b[l], 6, axis=-1)]
        m_ctx = jnp.split(jax.nn.silu(c_ctx) @ ada_w[l] + ada_b[l], 6, axis=-1)
        hl = modulate(rms_norm(x, norm_mix[l]), m_lat[0], m_lat[1])
        hc = modulate(rms_norm(h_ctx, norm_mix[l]), m_ctx[0], m_ctx[1])
        h_all = jnp.concatenate([hc[0], hl[0]], axis=0).astype(jnp.bfloat16)
        n_mix = D_IN - N_BRANCH * D_MODEL
        p_mix = pallas_matmul(h_all, w_in[l][:, :n_mix].astype(jnp.bfloat16))
        p_gate = pallas_matmul(h_all, w_in[l][:, n_mix:].astype(jnp.bfloat16), jnp.bfloat16)
        mix_cols = IN_COLS[:-1]
        qc_, kc_, vc_, uc, rwc, hgc = split_cols(p_mix[:C][None], mix_cols)
        ql_, kl_, vl_, ul, rwl, hgl = split_cols(p_mix[C:][None], mix_cols)
        att_c, att_l = attention_branch(qc_, kc_, vc_, ql_, kl_, vl_, qk_norm_q[l], qk_norm_k[l], att_sink[l],
                                        cos, sin, need_ctx)
        ssm_c, ssm_l = s5_branch(uc, ul, ssm_lam_re[l], ssm_lam_im[l], ssm_log_dt[l], ssm_b_re[l], ssm_b_im[l],
                                 ssm_c_re[l], ssm_c_im[l], ssm_d[l], ssm_glu_w[l], ssm_glu_b[l], need_ctx)
        rw_c, rw_l = rwkv_branch(rwc, rwl, rwkv_mu_prev[l], rwkv_mu_next[l], rwkv_w0[l], rwkv_w2[l], rwkv_a0[l],
                                 rwkv_a2[l], rwkv_g2[l], rwkv_k_k[l], rwkv_k_a[l], rwkv_r_k[l], rwkv_lnx_w[l],
                                 rwkv_lnx_b[l], need_ctx)
        hg_c, hg_l = hgrn_branch(hgc, hgl, lb_all[l], hgrn_norm[l], need_ctx)
        m_gate = jnp.stack([m_ctx[2], m_lat[2][0, 0]])
        lat_br = [t[0] for t in (att_l, ssm_l, rw_l, hg_l)]
        if need_ctx:
            ctx_br = [t[0] for t in (att_c, ssm_c, rw_c, hg_c)]
            rows = jnp.concatenate([h_ctx[0], x[0]], axis=0)
            merged = gated_merge_pallas(rows, [jnp.concatenate(p, axis=0) for p in zip(ctx_br, lat_br)], p_gate,
                                        b_merge[l], w_branch[l], w_out[l], m_gate, C)
            h_ctx, x = merged[:C][None], merged[C:][None]
        else:
            x = gated_merge_pallas(x[0], lat_br, p_gate, b_merge[l], w_branch[l], w_out[l], m_gate, 0, C)[None]
        hl2 = modulate(rms_norm(x, norm_ffn[l]), m_lat[3], m_lat[4])
        moe_args = (router_w[l], router_b[l], exp_gu_w[l], exp_gu_b[l], exp_down_w[l], exp_down_b[l])
        if need_ctx:
            hc2 = modulate(rms_norm(h_ctx, norm_ffn[l]), m_ctx[3], m_ctx[4])
            tokens = jnp.concatenate([hc2.reshape(B * C, D), hl2.reshape(B * S, D)], axis=0)
            y = moe_ffn(tokens, *moe_args)
            h_ctx = h_ctx + m_ctx[5] * y[:B * C].reshape(B, C, D)
            x = x + m_lat[5] * y[B * C:].reshape(B, S, D)
        else:
            y = moe_ffn(hl2.reshape(B * S, D), *moe_args)
            x = x + m_lat[5] * y.reshape(B, S, D)
    return x
```

```python
import functools
import math
import jax
import jax.numpy as jnp
from jax import lax
import numpy as np
from jax.experimental import pallas as pl
from jax.experimental.pallas import tpu as pltpu

D_MODEL = 2048
BATCH = 1
SEQ = 8192
DEPTH = 2

GRID_W = 64
CTX_LEN = 256
HEAD_DIM = 64
N_BRANCH = 4
BRANCH_WIDTH = D_MODEL // 4

ATT_HEADS = BRANCH_WIDTH // HEAD_DIM
ATT_KV_HEADS = ATT_HEADS // 4
ATT_GROUP = ATT_HEADS // ATT_KV_HEADS
WINDOW = 128
ATT_BLOCK = 128
ROPE_BASE = 10000.0

SSM_GROUP = 16
SSM_GROUPS = BRANCH_WIDTH // SSM_GROUP
SSM_STATE = 64

RWKV_HEADS = BRANCH_WIDTH // HEAD_DIM
DECAY_LORA = 64
ICLR_LORA = 64
GATE_LORA = 128
RWKV_GN_EPS = 64e-5

HGRN_HEADS = BRANCH_WIDTH // HEAD_DIM
HGRN_CHUNK = 64

N_EXPERTS = 32
TOP_K = 4
D_EXPERT = D_MODEL // 2
SWIGLU_LIMIT = 7.0
SWIGLU_ALPHA = 1.702
MOE_BLOCK = 128

NORM_EPS = 1e-6

ATT_COLS = (ATT_HEADS * HEAD_DIM, ATT_KV_HEADS * HEAD_DIM, ATT_KV_HEADS * HEAD_DIM)
RWKV_COLS = (BRANCH_WIDTH, BRANCH_WIDTH, BRANCH_WIDTH, DECAY_LORA, DECAY_LORA, ICLR_LORA, ICLR_LORA, GATE_LORA)
HGRN_COLS = (BRANCH_WIDTH,) * 5
IN_COLS = ATT_COLS + (BRANCH_WIDTH, sum(RWKV_COLS), sum(HGRN_COLS), N_BRANCH * D_MODEL)
D_IN = sum(IN_COLS)

VMEM_LIMIT_BYTES = 48 * 1024 * 1024
MOE_VMEM_LIMIT_BYTES = 56 * 1024 * 1024


def split_cols(p, sizes):
    idx = [sum(sizes[:i + 1]) for i in range(len(sizes) - 1)]
    return jnp.split(p, idx, axis=-1)


def rms_norm(x, w):
    xf = x.astype(jnp.float32)
    y = xf * lax.rsqrt(jnp.mean(xf * xf, axis=-1, keepdims=True) + NORM_EPS)
    return (y * w).astype(x.dtype)


def modulate(h, shift, scale):
    return h * (1.0 + scale) + shift


def axial_rope(rows):
    n_freq = HEAD_DIM // 4
    inv = ROPE_BASE ** (-jnp.arange(n_freq, dtype=jnp.float32) / n_freq)
    row = jnp.repeat(jnp.arange(rows, dtype=jnp.float32), GRID_W)
    col = jnp.tile(jnp.arange(GRID_W, dtype=jnp.float32), rows)
    ang = jnp.concatenate([row[:, None] * inv, col[:, None] * inv], axis=-1)
    return jnp.cos(ang), jnp.sin(ang)


def apply_rope(x, cos, sin):
    xf = x.astype(jnp.float32)
    half = HEAD_DIM // 2
    x1, x2 = xf[..., :half], xf[..., half:]
    cs, sn = cos[None, :, None, :], sin[None, :, None, :]
    return jnp.concatenate([x1 * cs - x2 * sn, x1 * sn + x2 * cs], axis=-1).astype(x.dtype)


def softmax_with_sink(logits, sink):
    m = jnp.maximum(jnp.max(logits, axis=-1, keepdims=True), sink)
    e = jnp.exp(logits - m)
    return e / (jnp.sum(e, axis=-1, keepdims=True) + jnp.exp(sink - m))


def attention_branch(q_c, k_c, v_c, q_l, k_l, v_l, qn_w, kn_w, sink, cos, sin, need_ctx):
    B, S = q_l.shape[:2]
    C = q_c.shape[1]
    heads = lambda t, h: t.reshape(t.shape[0], t.shape[1], h, HEAD_DIM)
    kc = rms_norm(heads(k_c, ATT_KV_HEADS), kn_w)
    vc = heads(v_c, ATT_KV_HEADS)
    ql = apply_rope(rms_norm(heads(q_l, ATT_HEADS), qn_w), cos, sin)
    kl = apply_rope(rms_norm(heads(k_l, ATT_KV_HEADS), kn_w), cos, sin)
    vl = heads(v_l, ATT_KV_HEADS)
    scale = HEAD_DIM ** -0.5
    sink_f = sink.astype(jnp.float32).reshape(ATT_KV_HEADS, ATT_GROUP)
    nb = S // ATT_BLOCK
    qb = ql.reshape(B, nb, ATT_BLOCK, ATT_KV_HEADS, ATT_GROUP, HEAD_DIM)

    def band(t):
        tp = jnp.pad(t, ((0, 0), (ATT_BLOCK, ATT_BLOCK), (0, 0), (0, 0)))
        tp = tp.reshape(B, nb + 2, ATT_BLOCK, ATT_KV_HEADS, HEAD_DIM)
        return jnp.concatenate([tp[:, :-2], tp[:, 1:-1], tp[:, 2:]], axis=2)

    kb, vb = band(kl), band(vl)
    s_win = jnp.einsum('bnqhgd,bnkhd->bnhgqk', qb, kb, preferred_element_type=jnp.float32) * scale
    q_pos = jnp.arange(nb)[:, None] * ATT_BLOCK + jnp.arange(ATT_BLOCK)[None, :]
    k_pos = jnp.arange(nb)[:, None] * ATT_BLOCK + jnp.arange(3 * ATT_BLOCK)[None, :] - ATT_BLOCK
    kp = k_pos[:, None, :]
    valid = (jnp.abs(kp - q_pos[:, :, None]) <= WINDOW) & (kp >= 0) & (kp < S)
    s_win = jnp.where(valid[None, :, None, None], s_win, -jnp.inf)
    s_ctx = jnp.einsum('bnqhgd,bkhd->bnhgqk', qb, kc, preferred_element_type=jnp.float32) * scale
    p = softmax_with_sink(jnp.concatenate([s_win, s_ctx], axis=-1), sink_f[None, None, :, :, None, None])
    p = p.astype(vl.dtype)
    o = (jnp.einsum('bnhgqk,bnkhd->bnqhgd', p[..., :3 * ATT_BLOCK], vb)
         + jnp.einsum('bnhgqk,bkhd->bnqhgd', p[..., 3 * ATT_BLOCK:], vc))
    lat = o.reshape(B, S, ATT_HEADS * HEAD_DIM)
    ctx_out = None
    if need_ctx:
        qc = rms_norm(heads(q_c, ATT_HEADS), qn_w).reshape(B, C, ATT_KV_HEADS, ATT_GROUP, HEAD_DIM)
        s_cc = jnp.einsum('bqhgd,bkhd->bhgqk', qc, kc, preferred_element_type=jnp.float32) * scale
        pc = softmax_with_sink(s_cc, sink_f[None, :, :, None, None]).astype(vc.dtype)
        ctx_out = jnp.einsum('bhgqk,bkhd->bqhgd', pc, vc).reshape(B, C, ATT_HEADS * HEAD_DIM)
    return ctx_out, lat


S5_BLOCK = 128


def _s5_scan_kernel(reverse, u_ref, bre_ref, bim_ref, pwr_ref, pwi_ref, cre_ref, cim_ref, y_ref, car_ref):
    TB = S5_BLOCK
    f32 = jnp.float32

    @pl.when(pl.program_id(0) == 0)
    def _():
        car_ref[...] = jnp.zeros_like(car_ref)

    u = u_ref[...].astype(jnp.bfloat16)
    xr = jnp.dot(u, bre_ref[...], preferred_element_type=f32)
    xi = jnp.dot(u, bim_ref[...], preferred_element_type=f32)
    rows = lax.broadcasted_iota(jnp.int32, (TB, 1), 0)
    s = 1
    while s < TB:
        p = TB - s if reverse else s - 1
        ar = pwr_ref[p:p + 1, :]
        ai = pwi_ref[p:p + 1, :]
        keep = rows < TB - s if reverse else rows >= s
        shift = TB - s if reverse else s
        sr = jnp.where(keep, pltpu.roll(xr, shift, axis=0), 0.0)
        si = jnp.where(keep, pltpu.roll(xi, shift, axis=0), 0.0)
        xr, xi = xr + (ar * sr - ai * si), xi + (ar * si + ai * sr)
        s *= 2
    cr = car_ref[0:1, :]
    ci = car_ref[1:2, :]
    pr = pwr_ref[...]
    pi_ = pwi_ref[...]
    xr = xr + (pr * cr - pi_ * ci)
    xi = xi + (pr * ci + pi_ * cr)
    last = 0 if reverse else TB - 1
    car_ref[0:1, :] = xr[last:last + 1, :]
    car_ref[1:2, :] = xi[last:last + 1, :]
    y_ref[...] = (jnp.dot(xr.astype(jnp.bfloat16), cre_ref[...], preferred_element_type=f32)
                  - jnp.dot(xi.astype(jnp.bfloat16), cim_ref[...], preferred_element_type=f32))


def s5_scan_pallas(u, bt_re, bt_im, pw_re, pw_im, c_re, c_im, reverse, n_ctx_blocks):
    T, W = u.shape
    N = bt_re.shape[-1]
    TB = S5_BLOCK
    nb = T // TB
    blk = (lambda i: (_backward_block(i, n_ctx_blocks, nb), 0)) if reverse else (lambda i: (i, 0))
    const = lambda i: (0, 0)
    return pl.pallas_call(
        functools.partial(_s5_scan_kernel, reverse),
        grid=(nb,),
        in_specs=[pl.BlockSpec((TB, W), blk),
                  pl.BlockSpec((W, N), const),
                  pl.BlockSpec((W, N), const),
                  pl.BlockSpec((TB, N), const),
                  pl.BlockSpec((TB, N), const),
                  pl.BlockSpec((N, W), const),
                  pl.BlockSpec((N, W), const)],
        out_specs=pl.BlockSpec((TB, W), blk),
        out_shape=jax.ShapeDtypeStruct((T, W), jnp.float32),
        scratch_shapes=[pltpu.VMEM((2, N), jnp.float32)],
        compiler_params=pltpu.CompilerParams(dimension_semantics=("arbitrary",),
                                             vmem_limit_bytes=VMEM_LIMIT_BYTES),
    )(u, bt_re, bt_im, pw_re, pw_im, c_re, c_im)


def s5_branch(u_c, u_l, lam_re, lam_im, log_dt, b_re, b_im, c_re, c_im, d_skip, glu_w, glu_b, need_ctx):
    f32 = jnp.float32
    assert u_l.shape[0] == 1
    C = u_c.shape[1]
    G, P, GC = SSM_GROUPS, SSM_STATE, SSM_GROUP
    dt = jnp.exp(log_dt.astype(f32))[..., None]
    lr, li = lam_re.astype(f32), lam_im.astype(f32)
    mag = jnp.exp(dt * lr)
    ab_re, ab_im = mag * jnp.cos(dt * li), mag * jnp.sin(dt * li)
    den = lr * lr + li * li
    gm_re = ((ab_re - 1.0) * lr + ab_im * li) / den
    gm_im = (ab_im * lr - (ab_re - 1.0) * li) / den
    eye_g = jnp.eye(G, dtype=f32)
    br = jnp.transpose(b_re.astype(f32), (0, 2, 1))
    bi = jnp.transpose(b_im.astype(f32), (0, 2, 1))
    gr, gi = gm_re[:, :, None, :], gm_im[:, :, None, :]
    blockdiag = lambda m: jnp.einsum('dgcp,gh->dgchp', m, eye_g).reshape(2, G * GC, G * P)
    bt_re = blockdiag(gr * br - gi * bi).astype(jnp.bfloat16)
    bt_im = blockdiag(gr * bi + gi * br).astype(jnp.bfloat16)
    n = jnp.arange(1, S5_BLOCK + 1, dtype=f32)[None, :, None, None]
    theta = dt * li
    theta = theta - (2.0 * math.pi) * jnp.round(theta / (2.0 * math.pi))
    pmag = jnp.exp(n * (dt * lr)[:, None])
    pw_re = (pmag * jnp.cos(n * theta[:, None])).reshape(2, S5_BLOCK, G * P)
    pw_im = (pmag * jnp.sin(n * theta[:, None])).reshape(2, S5_BLOCK, G * P)
    cbd = lambda m: jnp.einsum('gcp,gh->gphc', m.astype(f32), eye_g).reshape(G * P, G * GC).astype(jnp.bfloat16)
    uc, ul = u_c[0].astype(f32), u_l[0].astype(f32)
    u_all = jnp.concatenate([uc, ul], axis=0)
    cr_bd, ci_bd = cbd(c_re), cbd(c_im)
    yp = (s5_scan_pallas(u_all, bt_re[0], bt_im[0], pw_re[0], pw_im[0], cr_bd, ci_bd, False, C // S5_BLOCK)
          + s5_scan_pallas(u_all, bt_re[1], bt_im[1], pw_re[1, ::-1], pw_im[1, ::-1], cr_bd, ci_bd, True,
                           C // S5_BLOCK))

    def readout(y, u):
        y = y + d_skip * u.astype(f32)
        z = jax.nn.gelu(y) @ glu_w + glu_b
        za, zb = jnp.split(z, 2, axis=-1)
        return (za * jax.nn.sigmoid(zb)).astype(u.dtype)

    lat = readout(yp[C:][None], u_l)
    ctx_out = readout(yp[:C][None], u_c) if need_ctx else None
    return ctx_out, lat


def shift_mix(p, mu_prev, mu_next):
    zero = jnp.zeros_like(p[:, :1])
    prev = jnp.concatenate([zero, p[:, :-1]], axis=1)
    nxt = jnp.concatenate([p[:, 1:], zero], axis=1)
    return p + mu_prev * (prev - p) + mu_next * (nxt - p)


RWKV_CHUNK = 64
RWKV_INV_BLOCK = 16

_NN = (((1,), (0,)), ((), ()))
_NT = (((1,), (1,)), ((), ()))
_TN = (((0,), (0,)), ((), ()))


def _split_bf16(x):
    hi = x.astype(jnp.bfloat16)
    lo = (x - hi.astype(jnp.float32)).astype(jnp.bfloat16)
    return hi, lo


def _dot1(a, b, dims=_NN):
    return lax.dot_general(a.astype(jnp.bfloat16), b.astype(jnp.bfloat16), dims,
                           preferred_element_type=jnp.float32)


def _dot3(a, b, dims=_NN):
    ah, al = _split_bf16(a)
    bh, bl = _split_bf16(b)
    d = functools.partial(lax.dot_general, dimension_numbers=dims, preferred_element_type=jnp.float32)
    return d(ah, bh) + (d(ah, bl) + d(al, bh))


def _each(fn, *lists):
    return [fn(*args) for args in zip(*lists)]


def _rwkv_chunk_maps(lw, r, kk, be, kd, v, rev):
    L = RWKV_CHUNK
    f32 = jnp.float32
    row = lax.broadcasted_iota(jnp.int32, (L, L), 0)
    col = lax.broadcasted_iota(jnp.int32, (L, L), 1)
    ahead = jnp.where(rev, col - row, row - col)
    tril_i = ahead >= 0
    tril_s = ahead > 0
    blk = (row // RWKV_INV_BLOCK) == (col // RWKV_INV_BLOCK)
    eye = jnp.where(row == col, 1.0, 0.0).astype(f32)
    tri = jnp.where(tril_i, 1.0, 0.0).astype(jnp.bfloat16)
    dd = functools.partial(jnp.dot, preferred_element_type=f32)

    def cumsum(x):
        h1 = x.astype(jnp.bfloat16)
        r1 = x - h1.astype(f32)
        h2 = r1.astype(jnp.bfloat16)
        h3 = (r1 - h2.astype(f32)).astype(jnp.bfloat16)
        return dd(tri, h1) + (dd(tri, h2) + dd(tri, h3))

    c = _each(cumsum, lw)
    c_end = [jnp.where(rev, x[0:1, :], x[L - 1:L, :]) for x in c]
    e_c = [jnp.exp(x) for x in c]
    e_nc = [jnp.exp(-x) for x in c]
    e_end = [jnp.exp(x) for x in c_end]
    al = _each(lambda a, x, w: a * jnp.exp(x - w), kk, c, lw)
    rt = _each(lambda a, e: a * e, r, e_c)
    bb = _each(lambda a, e: a * e, be, e_nc)
    kb = _each(lambda a, e: a * e, kd, e_nc)
    bh = _each(lambda a, e: a * e, bb, e_end)
    kh = _each(lambda a, e: a * e, kb, e_end)
    lhs = _each(lambda a, b: jnp.concatenate([a, b], axis=0), al, rt)
    g1 = _each(lambda a, b: _dot1(a, b, _NT), lhs, bb)
    g2 = _each(lambda a, b: _dot1(a, b, _NT), lhs, kb)
    n = [jnp.where(tril_s, g[:L], 0.0) for g in g1]
    arb = [jnp.where(tril_i, g[L:], 0.0) for g in g1]
    aak = [jnp.where(tril_s, g[:L], 0.0) for g in g2]
    ark = [jnp.where(tril_i, g[L:], 0.0) for g in g2]
    d1 = [jnp.where(blk, x, 0.0) for x in n]
    o1 = _each(lambda a, b: a - b, n, d1)
    d2 = _each(_dot1, d1, d1)
    d4 = _each(_dot1, d2, d2)
    d8 = _each(_dot1, d4, d4)
    imd = [eye - x for x in d1]
    t1 = _each(lambda a, b: a + _dot1(a, b), imd, d2)
    t2 = _each(lambda a, b: a + _dot1(a, b), t1, d4)
    dinv = _each(lambda a, b: a + _dot1(a, b), t2, d8)
    e1 = _each(_dot1, dinv, o1)
    e2 = _each(_dot1, e1, e1)
    ime = [eye - x for x in e1]
    fm = _each(lambda a, b: a + _dot1(a, b), ime, e2)
    minv = _each(_dot1, fm, dinv)
    at = _each(_dot1, minv, al)
    av = _each(_dot1, aak, v)
    pm = _each(_dot1, minv, av)
    rh = _each(lambda a, b, x: a - _dot1(b, x), rt, arb, at)
    y0 = _each(lambda a, x, b, p: _dot1(a, x) - _dot1(b, p), ark, v, arb, pm)
    gt = _each(lambda e, b, a: eye * e - _dot1(b, a, _TN), e_end, bh, at)
    ht = _each(lambda k, x, b, p: _dot1(k, x, _TN) - _dot1(b, p, _TN), kh, v, bh, pm)
    return gt, ht, rh, y0


def _rwkv_prep_kernel(lw_ref, be_ref, kd_ref, r_ref, kk_ref, v_ref, gt_ref, ht_ref, rh_ref, y0_ref):
    rev = pl.program_id(0) == 1
    hs = range(RWKV_HEADS)
    gt, ht, rh, y0 = _rwkv_chunk_maps([lw_ref[0, h] for h in hs], [r_ref[h] for h in hs], [kk_ref[h] for h in hs],
                                      [be_ref[0, h] for h in hs], [kd_ref[0, h] for h in hs],
                                      [v_ref[h] for h in hs], rev)
    for h in hs:
        gt_ref[0, h, 0] = gt[h]
        ht_ref[0, h, 0] = ht[h]
        rh_ref[0, h, 0] = rh[h]
        y0_ref[0, h, 0] = y0[h]


def _rwkv_scan_kernel(gt_ref, ht_ref, rh_ref, y0_ref, yf_ref, yb_ref, st_ref):
    @pl.when(pl.program_id(0) == 0)
    def _():
        st_ref[...] = jnp.zeros_like(st_ref)

    for d, y_ref in enumerate((yf_ref, yb_ref)):
        for h in range(RWKV_HEADS):
            st = st_ref[d, h]
            y_ref[h] = _dot3(rh_ref[d, h, 0], st) + y0_ref[d, h, 0]
            st_ref[d, h] = _dot3(gt_ref[d, h, 0], st) + ht_ref[d, h, 0]


def _backward_block(i, n_ctx, n_all):
    return jnp.where(i < n_ctx, n_ctx - 1 - i, n_all - 1 - (i - n_ctx))


def rwkv_scan_pallas(lw, be, kd, r, kk, v, n_ctx_chunks):
    _, H, T, N = lw.shape
    L = RWKV_CHUNK
    nc = T // L
    chunk = lambda d, i: jnp.where(d == 0, i, _backward_block(i, n_ctx_chunks, nc))
    dir_spec = pl.BlockSpec((1, H, L, N), lambda d, i: (d, 0, chunk(d, i), 0))
    shared_spec = pl.BlockSpec((H, L, N), lambda d, i: (0, chunk(d, i), 0))
    map_spec = pl.BlockSpec((1, H, 1, N, N), lambda d, i: (d, 0, i, 0, 0))
    map_shape = jax.ShapeDtypeStruct((2, H, nc, N, N), jnp.float32)
    gt, ht, rh, y0 = pl.pallas_call(
        _rwkv_prep_kernel,
        grid=(2, nc),
        in_specs=[dir_spec] * 3 + [shared_spec] * 3,
        out_specs=[map_spec] * 4,
        out_shape=[map_shape] * 4,
        compiler_params=pltpu.CompilerParams(dimension_semantics=("parallel", "parallel"),
                                             vmem_limit_bytes=VMEM_LIMIT_BYTES),
    )(lw, be, kd, r, kk, v)
    scan_spec = pl.BlockSpec((2, H, 1, N, N), lambda i: (0, 0, i, 0, 0))
    y_shape = jax.ShapeDtypeStruct((H, T, N), jnp.float32)
    return pl.pallas_call(
        _rwkv_scan_kernel,
        grid=(nc,),
        in_specs=[scan_spec] * 4,
        out_specs=[pl.BlockSpec((H, L, N), lambda i: (0, i, 0)),
                   pl.BlockSpec((H, L, N), lambda i: (0, _backward_block(i, n_ctx_chunks, nc), 0))],
        out_shape=[y_shape, y_shape],
        scratch_shapes=[pltpu.VMEM((2, H, N, N), jnp.float32)],
        compiler_params=pltpu.CompilerParams(dimension_semantics=("arbitrary",),
                                             vmem_limit_bytes=VMEM_LIMIT_BYTES),
    )(gt, ht, rh, y0)


def rwkv_branch(p_c, p_l, mu_prev, mu_next, w0, w2, a0, a2, g2, k_k, k_a, r_k, lnx_w, lnx_b, need_ctx):
    f32 = jnp.float32
    W = BRANCH_WIDTH
    assert p_l.shape[0] == 1
    C = p_c.shape[1]

    def features(p):
        B, T = p.shape[:2]
        hd = lambda t: t.reshape(B, T, RWKV_HEADS, HEAD_DIM)
        pm = shift_mix(p.astype(f32), mu_prev, mu_next)
        r, k, v, w1f, w1b, a1f, a1b, g1 = split_cols(pm, RWKV_COLS)
        kk = hd(k * k_k)
        kk = kk * lax.rsqrt(jnp.maximum(jnp.sum(kk * kk, axis=-1, keepdims=True), 1e-12))
        dirs = []
        for d, (w1, a1) in enumerate(((w1f, a1f), (w1b, a1b))):
            w = -jax.nn.softplus(-(w0[d] + jnp.tanh(w1) @ w2[d])) - 0.5
            log_decay = -jnp.exp(w)
            a = jax.nn.sigmoid(a0[d] + a1 @ a2[d])
            k_d = k * (1.0 + (a - 1.0) * k_a)
            dirs.append((hd(log_decay), hd(k_d), hd(a)))
        g = jax.nn.sigmoid(g1) @ g2
        return hd(r), hd(v), kk, dirs, g

    def readout(y, r, v, dirs, g, dtype):
        B, T = y.shape[:2]
        mu = jnp.mean(y, axis=-1, keepdims=True)
        var = jnp.mean(jnp.square(y - mu), axis=-1, keepdims=True)
        yn = ((y - mu) * lax.rsqrt(var + RWKV_GN_EPS)).reshape(B, T, W) * lnx_w + lnx_b
        bonus = jnp.sum(r * (dirs[0][1] + dirs[1][1]) * r_k, axis=-1, keepdims=True) * v
        return ((yn + bonus.reshape(B, T, W)) * g).astype(dtype)

    rc, vc, kkc, dirs_c, gc = features(p_c)
    rl, vl, kkl, dirs_l, gl = features(p_l)

    def nat(tc, tl):
        return jnp.concatenate([tc[0], tl[0]], axis=0).transpose(1, 0, 2)

    both = lambda f: jnp.stack([f(0), f(1)])
    lw = both(lambda d: nat(dirs_c[d][0], dirs_l[d][0]))
    kd = both(lambda d: nat(dirs_c[d][1], dirs_l[d][1]))
    be = both(lambda d: nat(kkc * dirs_c[d][2], kkl * dirs_l[d][2]))
    y_f, y_b = rwkv_scan_pallas(lw, be, kd, nat(rc, rl), nat(kkc, kkl), nat(vc, vl), C // RWKV_CHUNK)
    y = (y_f + y_b).transpose(1, 0, 2)
    lat = readout(y[C:][None], rl, vl, dirs_l, gl, p_l.dtype)
    ctx_out = readout(y[:C][None], rc, vc, dirs_c, gc, p_c.dtype) if need_ctx else None
    return ctx_out, lat


HGRN_SUB = 16
HGRN_BLOCK = 128


def _hgrn_kernel(reverse, q_ref, lf_ref, k_ref, v_ref, seg_ref, o_ref, st_ref):
    TB, L = HGRN_BLOCK, HGRN_SUB
    f32 = jnp.float32
    bf16 = jnp.bfloat16

    @pl.when(pl.program_id(0) == 0)
    def _():
        st_ref[...] = jnp.zeros_like(st_ref)

    q = q_ref[...]
    k = k_ref[...]
    v = v_ref[...]
    seg = seg_ref[...]
    pos = lax.broadcasted_iota(jnp.int32, (TB, 1), 0) % L
    if reverse:
        pos = L - 1 - pos
    back = lambda x, j: pltpu.roll(x, TB - j if reverse else j, axis=0)
    b = lf_ref[...]
    s = 1
    while s < L:
        b = b + jnp.where(pos >= s, back(b, s), 0.0)
        s *= 2
    acc = jnp.zeros_like(q)
    for j in range(L):
        if j == 0:
            z = q * k
            vj = v
        else:
            keep = pos >= j
            ex = jnp.exp(jnp.where(keep, b - back(b, j), 0.0))
            z = jnp.where(keep, q * back(k, j) * ex, 0.0)
            vj = back(v, j)
        a = jnp.dot(z.astype(bf16), seg, preferred_element_type=f32)
        acc = acc + a * vj
    o_ref[...] = acc
    eb = jnp.exp(b)
    qs = (q * eb).astype(bf16)
    same_head = seg > 0
    n_sub = TB // L
    for step in range(n_sub):
        i = n_sub - 1 - step if reverse else step
        rows = slice(i * L, (i + 1) * L)
        last = i * L if reverse else (i + 1) * L - 1
        b_end = b[last:last + 1, :]
        st = st_ref[...]
        o_ref[rows, :] += lax.dot_general(qs[rows], st.astype(bf16), _NT, preferred_element_type=f32)
        kt = (k[rows] * jnp.exp(b_end - b[rows])).astype(bf16)
        upd = lax.dot_general(v[rows].astype(bf16), kt, _TN, preferred_element_type=f32)
        st_ref[...] = st * jnp.exp(b_end) + jnp.where(same_head, upd, 0.0)


def hgrn_scan_pallas(q, lf, k, v, reverse, n_ctx_blocks):
    T, W = q.shape
    TB = HGRN_BLOCK
    nb = T // TB
    head = jnp.arange(W, dtype=jnp.int32) // HEAD_DIM
    seg = (head[:, None] == head[None, :]).astype(jnp.bfloat16)
    blk = (lambda i: (_backward_block(i, n_ctx_blocks, nb), 0)) if reverse else (lambda i: (i, 0))
    spec = pl.BlockSpec((TB, W), blk)
    return pl.pallas_call(
        functools.partial(_hgrn_kernel, reverse),
        grid=(nb,),
        in_specs=[spec, spec, spec, spec, pl.BlockSpec((W, W), lambda i: (0, 0))],
        out_specs=spec,
        out_shape=jax.ShapeDtypeStruct((T, W), jnp.float32),
        scratch_shapes=[pltpu.VMEM((W, W), jnp.float32)],
        compiler_params=pltpu.CompilerParams(dimension_semantics=("arbitrary",),
                                             vmem_limit_bytes=VMEM_LIMIT_BYTES),
    )(q, lf, k, v, seg)


def hgrn_branch(p_c, p_l, lb, norm_w, need_ctx):
    f32 = jnp.float32
    log_lb, log_ub = jnp.log(lb), jnp.log1p(-lb)

    def features(p):
        B, T = p.shape[:2]
        hd = lambda t: t.reshape(B, T, HGRN_HEADS, HEAD_DIM)
        q, zf, zb, i, g = split_cols(p.astype(f32), HGRN_COLS)
        gates = []
        for z in (zf, zb):
            log_f = jnp.logaddexp(log_lb, log_ub + jax.nn.log_sigmoid(z))
            k = (1.0 - lb) * jax.nn.sigmoid(-z)
            gates.append((hd(log_f), hd(k)))
        return hd(q), hd(i), gates, g

    def readout(o, g, dtype):
        B, T = o.shape[:2]
        return (rms_norm(o, norm_w).reshape(B, T, BRANCH_WIDTH) * jax.nn.silu(g)).astype(dtype)

    qc, ic, gates_c, gc = features(p_c)
    ql, il, gates_l, gl = features(p_l)
    assert p_l.shape[0] == 1
    C = p_c.shape[1]
    W = BRANCH_WIDTH

    nat = lambda tc, tl: jnp.concatenate([tc[0], tl[0]], axis=0).reshape(-1, W)
    q_all, i_all = nat(qc, ql), nat(ic, il)
    o = sum(hgrn_scan_pallas(q_all, nat(gates_c[d][0], gates_l[d][0]), nat(gates_c[d][1], gates_l[d][1]), i_all,
                             bool(d), C // HGRN_BLOCK) for d in range(2))
    o = o.reshape(-1, HGRN_HEADS, HEAD_DIM)
    lat = readout(o[C:][None], gl, p_l.dtype)
    ctx_out = readout(o[:C][None], gc, p_c.dtype) if need_ctx else None
    return ctx_out, lat


MERGE_ROWS = 256


def _merge_kernel(x_ref, b0_ref, b1_ref, b2_ref, b3_ref, g_ref, bm_ref, wb_ref, wo_ref, m_ref, o_ref):
    f32 = jnp.float32
    D = x_ref.shape[1]
    acc = jnp.zeros(x_ref.shape, f32)
    for n, b_ref in enumerate((b0_ref, b1_ref, b2_ref, b3_ref)):
        proj = jnp.dot(b_ref[...].astype(jnp.bfloat16), wb_ref[n], preferred_element_type=f32)
        gate = jax.nn.sigmoid(g_ref[:, n * D:(n + 1) * D].astype(f32) + bm_ref[:, n * D:(n + 1) * D])
        acc = acc + gate * proj
    out = jnp.dot(acc.astype(jnp.bfloat16), wo_ref[...], preferred_element_type=f32)
    o_ref[...] = x_ref[...] + m_ref[0] * out


def gated_merge_pallas(x, branches, gate_cols, b_merge, w_branch, w_out, m_gate, n_first, gate_row0=0):
    M, D = x.shape
    W = branches[0].shape[1]
    R = MERGE_ROWS
    once = pl.Buffered(1)
    row = lambda i: (i, 0)
    const2 = lambda i: (0, 0)
    return pl.pallas_call(
        _merge_kernel,
        grid=(M // R,),
        in_specs=[pl.BlockSpec((R, D), row)] + [pl.BlockSpec((R, W), row)] * N_BRANCH + [
            pl.BlockSpec((R, N_BRANCH * D), lambda i: (i + gate_row0 // R, 0)),
            pl.BlockSpec((1, N_BRANCH * D), const2, pipeline_mode=once),
            pl.BlockSpec((N_BRANCH, W, D), lambda i: (0, 0, 0), pipeline_mode=once),
            pl.BlockSpec((D, D), const2, pipeline_mode=once),
            pl.BlockSpec((1, 1, D), lambda i: (jnp.where(i < n_first // R, 0, 1), 0, 0))],
        out_specs=pl.BlockSpec((R, D), row),
        out_shape=jax.ShapeDtypeStruct((M, D), jnp.float32),
        compiler_params=pltpu.CompilerParams(dimension_semantics=("parallel",),
                                             vmem_limit_bytes=VMEM_LIMIT_BYTES),
    )(x, *branches, gate_cols, b_merge[None, :], w_branch.astype(jnp.bfloat16), w_out.astype(jnp.bfloat16),
      m_gate[:, None, :])


MOE_ROWS = 256


def _moe_expert_kernel(be_ref, nb_ref, x_ref, wgu_ref, wd_ref, bgu_ref, bd_ref, sel_ref, y_ref):
    f32 = jnp.float32

    @pl.when(pl.program_id(0) < nb_ref[0])
    def _():
        gu = jnp.dot(x_ref[...], wgu_ref[0], preferred_element_type=f32) + bgu_ref[0]
        gate = jnp.minimum(gu, SWIGLU_LIMIT)
        up = jnp.clip(pltpu.roll(gu, gu.shape[1] - 1, axis=1), -SWIGLU_LIMIT, SWIGLU_LIMIT)
        glu = gate * jax.nn.sigmoid(SWIGLU_ALPHA * gate)
        act2 = ((up + 1.0) * glu).astype(jnp.bfloat16)
        act = jnp.dot(act2, sel_ref[...], preferred_element_type=f32).astype(jnp.bfloat16)
        y_ref[...] = (jnp.dot(act, wd_ref[0], preferred_element_type=f32) + bd_ref[0]).astype(y_ref.dtype)

    @pl.when(pl.program_id(0) >= nb_ref[0])
    def _():
        y_ref[...] = jnp.zeros_like(y_ref)


def moe_experts_pallas(xb, block_e, n_used, wgu, wd, bgu, bd):
    cap, D = xb.shape
    F2 = wgu.shape[-1]
    F = F2 // 2
    R = MOE_ROWS
    n_blocks = cap // R
    sel = (jnp.arange(F2, dtype=jnp.int32)[:, None] == 2 * jnp.arange(F, dtype=jnp.int32)[None, :]).astype(jnp.bfloat16)
    grid_spec = pltpu.PrefetchScalarGridSpec(
        num_scalar_prefetch=2,
        grid=(n_blocks,),
        in_specs=[pl.BlockSpec((R, D), lambda i, be, nb: (i, 0)),
                  pl.BlockSpec((1, D, F2), lambda i, be, nb: (be[i], 0, 0)),
                  pl.BlockSpec((1, F, D), lambda i, be, nb: (be[i], 0, 0)),
                  pl.BlockSpec((1, 1, F2), lambda i, be, nb: (be[i], 0, 0)),
                  pl.BlockSpec((1, 1, D), lambda i, be, nb: (be[i], 0, 0)),
                  pl.BlockSpec((F2, F), lambda i, be, nb: (0, 0))],
        out_specs=pl.BlockSpec((R, D), lambda i, be, nb: (i, 0)),
    )
    return pl.pallas_call(
        _moe_expert_kernel,
        grid_spec=grid_spec,
        out_shape=jax.ShapeDtypeStruct((cap, D), jnp.bfloat16),
        compiler_params=pltpu.CompilerParams(dimension_semantics=("arbitrary",),
                                             vmem_limit_bytes=MOE_VMEM_LIMIT_BYTES),
    )(block_e, n_used, xb, wgu, wd, bgu, bd, sel)


def moe_ffn(h, router_w, router_b, gu_w, gu_b, dn_w, dn_b):
    N, D = h.shape
    R = MOE_ROWS
    logits = (jnp.dot(h, router_w, precision=lax.Precision.HIGHEST) + router_b).astype(jnp.float32)
    top_l, top_e = lax.top_k(logits, TOP_K)
    weights = jax.nn.softmax(top_l, axis=-1)
    n_assign = N * TOP_K
    flat_e = top_e.reshape(-1)
    order = jnp.argsort(flat_e)
    tok_sorted = (jnp.arange(n_assign, dtype=jnp.int32) // TOP_K)[order]
    counts = jnp.bincount(flat_e, length=N_EXPERTS)
    padded = (counts + R - 1) // R * R
    start_sorted = jnp.cumsum(counts) - counts
    pad_end = jnp.cumsum(padded)
    start_pad = pad_end - padded
    n_blocks = -(-(n_assign + N_EXPERTS * (R - 1)) // R)
    cap = n_blocks * R
    block_e = jnp.minimum(jnp.searchsorted(pad_end, jnp.arange(n_blocks, dtype=jnp.int32) * R, side='right'),
                          N_EXPERTS - 1).astype(jnp.int32)
    e_slot = jnp.repeat(block_e, R, total_repeat_length=cap)
    j_slot = jnp.arange(cap, dtype=jnp.int32) - start_pad[e_slot].astype(jnp.int32)
    p_slot = jnp.clip(start_sorted[e_slot].astype(jnp.int32) + j_slot, 0, n_assign - 1)
    slot_tok = jnp.where(j_slot < counts[e_slot], tok_sorted[p_slot], N).astype(jnp.int32)
    rank = jnp.argsort(order).astype(jnp.int32) - start_sorted[flat_e].astype(jnp.int32)
    slot_of_assign = start_pad[flat_e].astype(jnp.int32) + rank
    n_used = (pad_end[-1] // R).astype(jnp.int32).reshape(1)
    h_pad = jnp.concatenate([h.astype(jnp.bfloat16), jnp.zeros((1, D), jnp.bfloat16)], axis=0)
    xb = lax.optimization_barrier(h_pad)[slot_tok]
    bf16 = jnp.bfloat16
    yb = moe_experts_pallas(xb, block_e, n_used, gu_w.astype(bf16), dn_w.astype(bf16),
                            gu_b[:, None, :], dn_b[:, None, :])
    picked = lax.optimization_barrier(yb[slot_of_assign]).reshape(N, TOP_K, D)
    return jnp.sum(picked.astype(h.dtype) * weights.astype(h.dtype)[:, :, None], axis=1)


def _mm_kernel(a_ref, b_ref, o_ref):
    o_ref[...] = jnp.dot(a_ref[...], b_ref[...], preferred_element_type=jnp.float32).astype(o_ref.dtype)


def _pick(n, cands):
    for c in cands:
        if n % c == 0:
            return c
    return n


def pallas_matmul(a, b, out_dtype=jnp.float32):
    M, K = a.shape
    N = b.shape[1]
    tm = _pick(M, (1056, 1024, 768, 512, 256, 128))
    tn = _pick(N, (1152, 1024, 640, 512, 256, 128))
    return pl.pallas_call(
        _mm_kernel,
        grid=(M // tm, N // tn),
        in_specs=[pl.BlockSpec((tm, K), lambda i, j: (i, 0)),
                  pl.BlockSpec((K, tn), lambda i, j: (0, j))],
        out_specs=pl.BlockSpec((tm, tn), lambda i, j: (i, j)),
        out_shape=jax.ShapeDtypeStruct((M, N), out_dtype),
        compiler_params=pltpu.CompilerParams(dimension_semantics=("parallel", "parallel"),
                                             vmem_limit_bytes=VMEM_LIMIT_BYTES),
    )(a, b)


def kernel(x, c, ctx, c_ctx, ada_w, ada_b, norm_mix, norm_ffn, w_in, b_merge, qk_norm_q, qk_norm_k, att_sink,
           ssm_lam_re, ssm_lam_im, ssm_log_dt, ssm_b_re, ssm_b_im, ssm_c_re, ssm_c_im, ssm_d, ssm_glu_w, ssm_glu_b,
           rwkv_mu_prev, rwkv_mu_next, rwkv_w0, rwkv_w2, rwkv_a0, rwkv_a2, rwkv_g2, rwkv_k_k, rwkv_k_a, rwkv_r_k,
           rwkv_lnx_w, rwkv_lnx_b, hgrn_lb_raw, hgrn_norm, w_branch, w_out, router_w, router_b,
           exp_gu_w, exp_gu_b, exp_down_w, exp_down_b):
    B, S, D = x.shape
    C = ctx.shape[1]
    rows = S // GRID_W
    cos, sin = axial_rope(rows)
    lb_all = jnp.cumsum(jax.nn.softmax(hgrn_lb_raw.astype(jnp.float32), axis=0), axis=0)
    lb_all = lb_all - lb_all[:1]
    h_ctx = ctx
    for l in range(DEPTH):
        need_ctx = l < DEPTH - 1
        m_lat = [t[:, None, :] for t in jnp.split(jax.nn.silu(c) @ ada_w[l] + ada_b[l], 6, axis=-1)]
        m_ctx = jnp.split(jax.nn.silu(c_ctx) @ ada_w[l] + ada_b[l], 6, axis=-1)
        hl = modulate(rms_norm(x, norm_mix[l]), m_lat[0], m_lat[1])
        hc = modulate(rms_norm(h_ctx, norm_mix[l]), m_ctx[0], m_ctx[1])
        h_all = jnp.concatenate([hc[0], hl[0]], axis=0).astype(jnp.bfloat16)
        n_mix = D_IN - N_BRANCH * D_MODEL
        p_mix = pallas_matmul(h_all, w_in[l][:, :n_mix].astype(jnp.bfloat16))
        p_gate = pallas_matmul(h_all, w_in[l][:, n_mix:].astype(jnp.bfloat16), jnp.bfloat16)
        mix_cols = IN_COLS[:-1]
        qc_, kc_, vc_, uc, rwc, hgc = split_cols(p_mix[:C][None], mix_cols)
        ql_, kl_, vl_, ul, rwl, hgl = split_cols(p_mix[C:][None], mix_cols)
        att_c, att_l = attention_branch(qc_, kc_, vc_, ql_, kl_, vl_, qk_norm_q[l], qk_norm_k[l], att_sink[l],
                                        cos, sin, need_ctx)
        ssm_c, ssm_l = s5_branch(uc, ul, ssm_lam_re[l], ssm_lam_im[l], ssm_log_dt[l], ssm_b_re[l], ssm_b_im[l],
                                 ssm_c_re[l], ssm_c_im[l], ssm_d[l], ssm_glu_w[l], ssm_glu_b[l], need_ctx)
        rw_c, rw_l = rwkv_branch(rwc, rwl, rwkv_mu_prev[l], rwkv_mu_next[l], rwkv_w0[l], rwkv_w2[l], rwkv_a0[l],
                                 rwkv_a2[l], rwkv_g2[l], rwkv_k_k[l], rwkv_k_a[l], rwkv_r_k[l], rwkv_lnx_w[l],
                                 rwkv_lnx_b[l], need_ctx)
        hg_c, hg_l = hgrn_branch(hgc, hgl, lb_all[l], hgrn_norm[l], need_ctx)
        m_gate = jnp.stack([m_ctx[2], m_lat[2][0, 0]])
        lat_br = [t[0] for t in (att_l, ssm_l, rw_l, hg_l)]
        if need_ctx:
            ctx_br = [t[0] for t in (att_c, ssm_c, rw_c, hg_c)]
            rows = jnp.concatenate([h_ctx[0], x[0]], axis=0)
            merged = gated_merge_pallas(rows, [jnp.concatenate(p, axis=0) for p in zip(ctx_br, lat_br)], p_gate,
                                        b_merge[l], w_branch[l], w_out[l], m_gate, C)
            h_ctx, x = merged[:C][None], merged[C:][None]
        else:
            x = gated_merge_pallas(x[0], lat_br, p_gate, b_merge[l], w_branch[l], w_out[l], m_gate, 0, C)[None]
        hl2 = modulate(rms_norm(x, norm_ffn[l]), m_lat[3], m_lat[4])
        moe_args = (router_w[l], router_b[l], exp_gu_w[l], exp_gu_b[l], exp_down_w[l], exp_down_b[l])
        if need_ctx:
            hc2 = modulate(rms_norm(h_ctx, norm_ffn[l]), m_ctx[3], m_ctx[4])
            tokens = jnp.concatenate([hc2.reshape(B * C, D), hl2.reshape(B * S, D)], axis=0)
            y = moe_ffn(tokens, *moe_args)
            h_ctx = h_ctx + m_ctx[5] * y[:B * C].reshape(B, C, D)
            x = x + m_lat[5] * y[B * C:].reshape(B, S, D)
        else:
            y = moe_ffn(hl2.reshape(B * S, D), *moe_args)
            x = x + m_lat[5] * y.reshape(B, S, D)
    return x
```
